```python
import math
import jax, jax.numpy as jnp
from jax import lax
import numpy as np

D_MODEL = 2048
BATCH = 4
SEQ = 2048
DEPTH = 2
DEC_BATCH = 8
DEC_SEQ = 8
PAST_LEN = 16384
PAGE_SIZE = 128

N_MIXERS = 2
N_GMLP_LAYERS = (DEPTH + 1) // 2
N_ATTN_LAYERS = DEPTH // 2
DN_ALPHA = (2 * DEPTH) ** 0.25
DN_BETA = (8 * DEPTH) ** -0.25
LN_EPS = 1e-5
CHUNK = 128
GM_INNER = D_MODEL
GM_GROUPS = 16
GM_GROUP_W = GM_INNER // GM_GROUPS
N_HEADS = 16
HEAD_DIM = D_MODEL // (2 * N_HEADS)
V_DIM = 2 * HEAD_DIM
ROT_DIM = HEAD_DIM // 4
ROPE_THETA = 500000.0
Q_BLOCK = 128
D_FF = -(-8 * D_MODEL // (3 * 256)) * 256

kernel_name = "hybrid_gmlp_diffattn_deepnorm_step"


def layer_norm(x, g, b):
    xf = x.astype(jnp.float32)
    mu = xf.mean(-1, keepdims=True)
    var = jnp.square(xf - mu).mean(-1, keepdims=True)
    return ((xf - mu) * lax.rsqrt(var + LN_EPS) * g + b).astype(x.dtype)


def rms_norm(x, g):
    xf = x.astype(jnp.float32)
    return (xf * lax.rsqrt(jnp.square(xf).mean(-1, keepdims=True) + LN_EPS) * g).astype(x.dtype)


def partial_rope(x, pos):
    half = ROT_DIM // 2
    inv = jnp.power(ROPE_THETA, -jnp.arange(half, dtype=jnp.float32) * 2.0 / ROT_DIM)
    ang = pos.astype(jnp.float32)[:, None] * inv[None, :]
    cos = jnp.cos(ang)[None, :, None, None, :]
    sin = jnp.sin(ang)[None, :, None, None, :]
    xf = x.astype(jnp.float32)
    x1, x2 = xf[..., :half], xf[..., half:ROT_DIM]
    return jnp.concatenate([x1 * cos - x2 * sin, x1 * sin + x2 * cos, xf[..., ROT_DIM:]], axis=-1).astype(x.dtype)


def gmlp_mixer(h, w_in, b_in, ln_g, ln_b, w_s, b_s, w_out, b_out):
    B, L, _ = h.shape
    z = jax.nn.gelu(h @ w_in + b_in, approximate=False)
    u, v = jnp.split(z, 2, axis=-1)
    v = layer_norm(v, ln_g, ln_b)
    pad = (-L) % CHUNK
    n_chunks = (L + pad) // CHUNK
    vc = jnp.pad(v, ((0, 0), (0, pad), (0, 0))).reshape(B, n_chunks, CHUNK, GM_GROUPS, GM_GROUP_W)
    w_causal = w_s * jnp.tril(jnp.ones((CHUNK, CHUNK), w_s.dtype))
    mixed = jnp.einsum("gts,bnsgc->bntgc", w_causal, vc) + b_s.T[None, None, :, :, None]
    mixed = mixed.reshape(B, n_chunks * CHUNK, GM_INNER)[:, :L]
    return (u * mixed) @ w_out + b_out, v


def diff_lambda(lq1, lk1, lq2, lk2, lam_init):
    def e(a, b):
        return jnp.exp(jnp.sum(a.astype(jnp.float32) * b.astype(jnp.float32)))
    return e(lq1, lk1) - e(lq2, lk2) + lam_init


def diff_project(h, w_qkv, pos):
    B, L, _ = h.shape
    q, k, v = jnp.split(h @ w_qkv, 3, axis=-1)
    q = partial_rope(q.reshape(B, L, N_HEADS, 2, HEAD_DIM), pos)
    k = partial_rope(k.reshape(B, L, N_HEADS, 2, HEAD_DIM), pos)
    return q, k, v.reshape(B, L, N_HEADS, V_DIM)


def diff_attention(q, k, v, q_pos, k_pos, lam):
    B, Lq = q.shape[0], q.shape[1]
    qb = min(Q_BLOCK, Lq)
    nb = Lq // qb
    q_blocks = jnp.moveaxis(q.reshape(B, nb, qb, N_HEADS, 2, HEAD_DIM), 1, 0)
    pos_blocks = q_pos.reshape(nb, qb)
    scale = HEAD_DIM ** -0.5

    def one_block(args):
        qblk, pblk = args
        s = jnp.einsum("bqhmd,bkhmd->bhmqk", qblk, k, preferred_element_type=jnp.float32) * scale
        s = jnp.where((k_pos[None, :] <= pblk[:, None])[None, None, None], s, -jnp.inf)
        p = jax.nn.softmax(s, axis=-1)
        a = p[:, :, 0] - lam * p[:, :, 1]
        return jnp.einsum("bhqk,bkhe->bqhe", a.astype(v.dtype), v)

    out = lax.map(one_block, (q_blocks, pos_blocks))
    return jnp.moveaxis(out, 0, 1).reshape(B, Lq, N_HEADS, V_DIM)


def diff_output(o, subln_g, lam_init, w_out):
    B, L = o.shape[0], o.shape[1]
    o = rms_norm(o, subln_g) * (1.0 - lam_init)
    return o.reshape(B, L, N_HEADS * V_DIM) @ w_out


def swiglu(h, w_in, w_out):
    g, u = jnp.split(h @ w_in, 2, axis=-1)
    return (jax.nn.silu(g) * u) @ w_out


def setup_inputs(seed: int = 0) -> dict:
    key = jax.random.key(seed)
    ks = jax.random.split(key, 27)
    f32 = jnp.float32

    def nrm(k, shape, scale):
        return jax.random.normal(k, shape, f32) * scale

    n_pages = PAST_LEN // PAGE_SIZE
    n_used = DEC_BATCH * n_pages
    n_pool = n_used + n_used // 4
    page_table = jax.random.permutation(ks[4], n_pool)[:n_used].reshape(DEC_BATCH, n_pages).astype(jnp.int32)
    nG, nA = N_GMLP_LAYERS, N_ATTN_LAYERS
    qk_w = nrm(ks[13], (nA, D_MODEL, 2 * N_HEADS * 2 * HEAD_DIM), D_MODEL ** -0.5)
    v_w = nrm(ks[14], (nA, D_MODEL, N_HEADS * V_DIM), DN_BETA * D_MODEL ** -0.5)
    return {
        "x_prompt": nrm(ks[0], (BATCH, SEQ, D_MODEL), 1.0),
        "x_sample": nrm(ks[1], (DEC_BATCH, DEC_SEQ, D_MODEL), 1.0),
        "cache_k": nrm(ks[2], (nA, n_pool, PAGE_SIZE, N_HEADS, 2 * HEAD_DIM), 1.0),
        "cache_v": nrm(ks[3], (nA, n_pool, PAGE_SIZE, N_HEADS, V_DIM), 1.0),
        "page_table": page_table,
        "gm_w_in": nrm(ks[5], (nG, D_MODEL, 2 * GM_INNER), DN_BETA * D_MODEL ** -0.5),
        "gm_b_in": nrm(ks[6], (nG, 2 * GM_INNER), 0.02),
        "gm_ln_g": 1.0 + nrm(ks[7], (nG, GM_INNER), 0.02),
        "gm_ln_b": nrm(ks[8], (nG, GM_INNER), 0.02),
        "gm_w_s": nrm(ks[9], (nG, GM_GROUPS, CHUNK, CHUNK), CHUNK ** -0.5),
        "gm_b_s": 1.0 + nrm(ks[10], (nG, GM_GROUPS, CHUNK), 0.1),
        "gm_w_out": nrm(ks[11], (nG, GM_INNER, D_MODEL), DN_BETA * GM_INNER ** -0.5),
        "gm_b_out": nrm(ks[12], (nG, D_MODEL), 0.02),
        "at_w_qkv": jnp.concatenate([qk_w, v_w], axis=-1),
        "at_lambda_q1": nrm(ks[15], (nA, HEAD_DIM), 0.1),
        "at_lambda_k1": nrm(ks[16], (nA, HEAD_DIM), 0.1),
        "at_lambda_q2": nrm(ks[17], (nA, HEAD_DIM), 0.1),
        "at_lambda_k2": nrm(ks[18], (nA, HEAD_DIM), 0.1),
        "at_subln_g": 1.0 + nrm(ks[19], (nA, V_DIM), 0.02),
        "at_w_out": nrm(ks[20], (nA, N_HEADS * V_DIM, D_MODEL), DN_BETA * (N_HEADS * V_DIM) ** -0.5),
        "ln_mix_g": 1.0 + nrm(ks[21], (DEPTH, D_MODEL), 0.02),
        "ln_mix_b": nrm(ks[22], (DEPTH, D_MODEL), 0.02),
        "ln_ffn_g": 1.0 + nrm(ks[23], (DEPTH, D_MODEL), 0.02),
        "ln_ffn_b": nrm(ks[24], (DEPTH, D_MODEL), 0.02),
        "ffn_w_in": nrm(ks[25], (DEPTH, D_MODEL, 2 * D_FF), DN_BETA * D_MODEL ** -0.5),
        "ffn_w_out": nrm(ks[26], (DEPTH, D_FF, D_MODEL), DN_BETA * D_FF ** -0.5),
    }


def reference(x_prompt, x_sample, cache_k, cache_v, page_table,
              gm_w_in, gm_b_in, gm_ln_g, gm_ln_b, gm_w_s, gm_b_s, gm_w_out, gm_b_out,
              at_w_qkv, at_lambda_q1, at_lambda_k1, at_lambda_q2, at_lambda_k2, at_subln_g, at_w_out,
              ln_mix_g, ln_mix_b, ln_ffn_g, ln_ffn_b, ffn_w_in, ffn_w_out):
    n_dec, n_pages = page_table.shape
    past_len = n_pages * cache_k.shape[2]
    b_prompt, seq = x_prompt.shape[0], x_prompt.shape[1]
    dec_seq = x_sample.shape[1]
    pos_p = jnp.arange(seq, dtype=jnp.int32)
    pos_s = past_len + jnp.arange(dec_seq, dtype=jnp.int32)
    kpos_s = jnp.arange(past_len + dec_seq, dtype=jnp.int32)
    open_start = ((seq - 1) // CHUNK) * CHUNK

    xp, xs = x_prompt, x_sample
    gm_v_p, gm_v_s, k_p, v_p, k_s, v_s = [], [], [], [], [], []
    for i in range(DEPTH):
        j = i // N_MIXERS
        if i % N_MIXERS == 0:
            gm = (gm_w_in[j], gm_b_in[j], gm_ln_g[j], gm_ln_b[j], gm_w_s[j], gm_b_s[j], gm_w_out[j], gm_b_out[j])
            mp, gv_p = gmlp_mixer(xp, *gm)
            ms, gv_s = gmlp_mixer(xs, *gm)
            gm_v_p.append(gv_p[:, open_start:])
            gm_v_s.append(gv_s)
        else:
            lam_init = 0.8 - 0.6 * math.exp(-0.3 * i)
            lam = diff_lambda(at_lambda_q1[j], at_lambda_k1[j], at_lambda_q2[j], at_lambda_k2[j], lam_init)
            qp, kp, vp = diff_project(xp, at_w_qkv[j], pos_p)
            mp = diff_output(diff_attention(qp, kp, vp, pos_p, pos_p, lam), at_subln_g[j], lam_init, at_w_out[j])
            qs, kn, vn = diff_project(xs, at_w_qkv[j], pos_s)
            past_k = cache_k[j][page_table].reshape(n_dec, past_len, N_HEADS, 2, HEAD_DIM)
            past_v = cache_v[j][page_table].reshape(n_dec, past_len, N_HEADS, V_DIM)
            k_all = jnp.concatenate([past_k, kn], axis=1)
            v_all = jnp.concatenate([past_v, vn], axis=1)
            ms = diff_output(diff_attention(qs, k_all, v_all, pos_s, kpos_s, lam), at_subln_g[j], lam_init, at_w_out[j])
            k_p.append(kp.reshape(b_prompt, seq, N_HEADS, 2 * HEAD_DIM))
            v_p.append(vp)
            k_s.append(kn.reshape(n_dec, dec_seq, N_HEADS, 2 * HEAD_DIM))
            v_s.append(vn)
        xp = layer_norm(DN_ALPHA * xp + mp, ln_mix_g[i], ln_mix_b[i])
        xs = layer_norm(DN_ALPHA * xs + ms, ln_mix_g[i], ln_mix_b[i])
        xp = layer_norm(DN_ALPHA * xp + swiglu(xp, ffn_w_in[i], ffn_w_out[i]), ln_ffn_g[i], ln_ffn_b[i])
        xs = layer_norm(DN_ALPHA * xs + swiglu(xs, ffn_w_in[i], ffn_w_out[i]), ln_ffn_g[i], ln_ffn_b[i])

    return (xp, xs, jnp.stack(gm_v_p), jnp.stack(gm_v_s), jnp.stack(k_p), jnp.stack(v_p), jnp.stack(k_s), jnp.stack(v_s))
```

```python
import functools
import math

import jax
import jax.numpy as jnp
from jax import lax
from jax.experimental import pallas as pl
from jax.experimental.pallas import tpu as pltpu

F32 = jnp.float32
BF16 = jnp.bfloat16

LN_EPS = 1e-5
CHUNK = 128
GROUP_W = 128
HEAD_DIM = 64
HEAD_W = 2 * HEAD_DIM
ROT_DIM = HEAD_DIM // 4
ROPE_THETA = 500000.0
LANES = 128
VMEM_LIMIT_BYTES = 60 * 1024 * 1024


def _params(*sem):
    return pltpu.CompilerParams(dimension_semantics=sem, vmem_limit_bytes=VMEM_LIMIT_BYTES)


def _layer_norm(y, g, b):
    mu = jnp.mean(y, axis=-1, keepdims=True)
    yc = y - mu
    var = jnp.mean(yc * yc, axis=-1, keepdims=True)
    return yc * lax.rsqrt(var + LN_EPS) * g + b


def _residual_ln_inplace(o_ref, x_ref, bias_ref, g_ref, b_ref, alpha, tm):
    rows_per_iter = 16

    def body(r, carry):
        rows = pl.ds(pl.multiple_of(r * rows_per_iter, rows_per_iter), rows_per_iter)
        y = alpha * x_ref[rows, :] + o_ref[rows, :]
        if bias_ref is not None:
            y = y + bias_ref[...]
        o_ref[rows, :] = _layer_norm(y, g_ref[...], b_ref[...])
        return carry

    lax.fori_loop(0, tm // rows_per_iter, body, 0)


def _gmlp_project(x_ref, w_ref, b_ref, xb_s, z_s):
    j = pl.program_id(1)

    @pl.when(j == 0)
    def _():
        xb_s[...] = x_ref[...].astype(BF16)

    z = jnp.dot(xb_s[...], w_ref[...].astype(BF16), preferred_element_type=F32) + b_ref[...]
    z_s[j] = 0.5 * z * (1.0 + lax.erf(z * math.sqrt(0.5)))


def _gmlp_v_layer_norm(z_s, lng_ref, lnb_ref, rows, n_half, tn, inner):
    vs = [z_s[n_half + c, rows, :] for c in range(n_half)]
    mu = sum(jnp.sum(v, axis=-1, keepdims=True) for v in vs) / inner
    var = sum(jnp.sum(jnp.square(v - mu), axis=-1, keepdims=True) for v in vs) / inner
    rstd = lax.rsqrt(var + LN_EPS)
    return [(vs[c] - mu) * rstd * lng_ref[:, c * tn:(c + 1) * tn] + lnb_ref[:, c * tn:(c + 1) * tn]
            for c in range(n_half)]


def _gmlp_in_prompt_kernel(x_ref, w_ref, b_ref, lng_ref, lnb_ref, ws_ref, bexp_ref,
                           gated_ref, gmv_ref, xb_s, z_s, vn_s, *, tm, tn, inner):
    _gmlp_project(x_ref, w_ref, b_ref, xb_s, z_s)
    n_half = inner // tn
    n_groups = inner // GROUP_W
    n_rc = tm // CHUNK
    per_chunk = tn // GROUP_W

    @pl.when(pl.program_id(1) == pl.num_programs(1) - 1)
    def _():
        for r in range(n_rc):
            rows = slice(r * CHUNK, (r + 1) * CHUNK)
            vn = _gmlp_v_layer_norm(z_s, lng_ref, lnb_ref, rows, n_half, tn, inner)
            for c in range(n_half):
                if r == n_rc - 1:
                    gmv_ref[:, c * tn:(c + 1) * tn] = vn[c]
                for q in range(per_chunk):
                    g = c * per_chunk + q
                    vn_s[g, :, r * CHUNK:(r + 1) * CHUNK] = vn[c][:, q * GROUP_W:(q + 1) * GROUP_W].astype(BF16)
        t_idx = lax.broadcasted_iota(jnp.int32, (CHUNK, CHUNK), 0)
        s_idx = lax.broadcasted_iota(jnp.int32, (CHUNK, CHUNK), 1)
        causal = s_idx <= t_idx
        for g in range(n_groups):
            w_causal = jnp.where(causal, ws_ref[g], 0.0).astype(BF16)
            mixed = jnp.dot(w_causal, vn_s[g], preferred_element_type=F32)
            cols = slice(g * GROUP_W, (g + 1) * GROUP_W)
            c, q = divmod(g, per_chunk)
            for r in range(n_rc):
                rows = slice(r * CHUNK, (r + 1) * CHUNK)
                u = z_s[c, rows, q * GROUP_W:(q + 1) * GROUP_W]
                m = mixed[:, r * CHUNK:(r + 1) * CHUNK] + bexp_ref[:, cols]
                gated_ref[rows, cols] = (u * m).astype(BF16)


def _gmlp_in_prompt(x, w_in, b_in, ln_g, ln_b, w_s, bexp, layer, seq, *, tm, tn):
    m_rows, d_model = x.shape
    inner = w_in.shape[2] // 2
    n_groups = inner // GROUP_W
    n_batch = m_rows // seq
    assert seq % tm == 0 and tm % CHUNK == 0 and inner % tn == 0 and tn % GROUP_W == 0
    tiles_per_seq = seq // tm
    nj = 2 * inner // tn
    kern = functools.partial(_gmlp_in_prompt_kernel, tm=tm, tn=tn, inner=inner)
    return pl.pallas_call(
        kern,
        grid=(m_rows // tm, nj),
        in_specs=[
            pl.BlockSpec((tm, d_model), lambda i, j: (i, 0)),
            pl.BlockSpec((None, d_model, tn), lambda i, j: (layer, 0, j)),
            pl.BlockSpec((None, 1, tn), lambda i, j: (layer, 0, j)),
            pl.BlockSpec((None, 1, inner), lambda i, j: (layer, 0, 0)),
            pl.BlockSpec((None, 1, inner), lambda i, j: (layer, 0, 0)),
            pl.BlockSpec((None, n_groups, CHUNK, CHUNK), lambda i, j: (layer, 0, 0, 0)),
            pl.BlockSpec((CHUNK, inner), lambda i, j: (0, 0)),
        ],
        out_specs=[
            pl.BlockSpec((tm, inner), lambda i, j: (i, 0)),
            pl.BlockSpec((None, CHUNK, inner), lambda i, j: (i // tiles_per_seq, 0, 0)),
        ],
        out_shape=[
            jax.ShapeDtypeStruct((m_rows, inner), BF16),
            jax.ShapeDtypeStruct((n_batch, CHUNK, inner), F32),
        ],
        scratch_shapes=[
            pltpu.VMEM((tm, d_model), BF16),
            pltpu.VMEM((nj, tm, tn), F32),
            pltpu.VMEM((n_groups, CHUNK, tm), BF16),
        ],
        compiler_params=_params("arbitrary", "arbitrary"),
    )(x, w_in, b_in.reshape(b_in.shape[0], 1, -1), ln_g.reshape(ln_g.shape[0], 1, -1),
      ln_b.reshape(ln_b.shape[0], 1, -1), w_s, bexp)


def _gmlp_in_sample_kernel(x_ref, w_ref, b_ref, lng_ref, lnb_ref, wexp_ref, bexp_ref,
                           gated_ref, gmv_ref, xb_s, z_s, *, tm, tn, inner, dec_seq):
    _gmlp_project(x_ref, w_ref, b_ref, xb_s, z_s)
    n_half = inner // tn

    @pl.when(pl.program_id(1) == pl.num_programs(1) - 1)
    def _():
        vn = _gmlp_v_layer_norm(z_s, lng_ref, lnb_ref, slice(0, tm), n_half, tn, inner)
        t_idx = lax.broadcasted_iota(jnp.int32, (dec_seq, tn), 0)
        for c in range(n_half):
            cols = slice(c * tn, (c + 1) * tn)
            gmv_ref[:, cols] = vn[c]
            for b in range(tm // dec_seq):
                rows = slice(b * dec_seq, (b + 1) * dec_seq)
                vb = vn[c][rows, :]
                mixed = bexp_ref[:, cols]
                for s in range(dec_seq):
                    w_ts = jnp.where(t_idx >= s, wexp_ref[s, :, cols], 0.0)
                    mixed = mixed + w_ts * vb[s:s + 1, :]
                gated_ref[rows, cols] = (z_s[c, rows, :] * mixed).astype(BF16)


def _gmlp_in_sample(x, w_in, b_in, ln_g, ln_b, wexp, bexp, layer, dec_seq, *, tn):
    m_rows, d_model = x.shape
    inner = w_in.shape[2] // 2
    assert dec_seq % 8 == 0 and dec_seq <= CHUNK and inner % tn == 0
    nj = 2 * inner // tn
    kern = functools.partial(_gmlp_in_sample_kernel, tm=m_rows, tn=tn, inner=inner, dec_seq=dec_seq)
    return pl.pallas_call(
        kern,
        grid=(1, nj),
        in_specs=[
            pl.BlockSpec((m_rows, d_model), lambda i, j: (0, 0)),
            pl.BlockSpec((None, d_model, tn), lambda i, j: (layer, 0, j)),
            pl.BlockSpec((None, 1, tn), lambda i, j: (layer, 0, j)),
            pl.BlockSpec((None, 1, inner), lambda i, j: (layer, 0, 0)),
            pl.BlockSpec((None, 1, inner), lambda i, j: (layer, 0, 0)),
            pl.BlockSpec((dec_seq, dec_seq, inner), lambda i, j: (0, 0, 0)),
            pl.BlockSpec((dec_seq, inner), lambda i, j: (0, 0)),
        ],
        out_specs=[
            pl.BlockSpec((m_rows, inner), lambda i, j: (0, 0)),
            pl.BlockSpec((m_rows, inner), lambda i, j: (0, 0)),
        ],
        out_shape=[
            jax.ShapeDtypeStruct((m_rows, inner), BF16),
            jax.ShapeDtypeStruct((m_rows, inner), F32),
        ],
        scratch_shapes=[
            pltpu.VMEM((m_rows, d_model), BF16),
            pltpu.VMEM((nj, m_rows, tn), F32),
        ],
        compiler_params=_params("arbitrary", "arbitrary"),
    )(x, w_in, b_in.reshape(b_in.shape[0], 1, -1), ln_g.reshape(ln_g.shape[0], 1, -1),
      ln_b.reshape(ln_b.shape[0], 1, -1), wexp, bexp)


def _proj_ln_kernel(*refs, alpha, tm, has_bias):
    if has_bias:
        a_ref, w_ref, bias_ref, x_ref, g_ref, b_ref, o_ref = refs
    else:
        a_ref, w_ref, x_ref, g_ref, b_ref, o_ref = refs
        bias_ref = None
    k = pl.program_id(1)
    part = jnp.dot(a_ref[...].astype(BF16), w_ref[...].astype(BF16), preferred_element_type=F32)

    @pl.when(k == 0)
    def _():
        o_ref[...] = part

    @pl.when(k > 0)
    def _():
        o_ref[...] += part

    @pl.when(k == pl.num_programs(1) - 1)
    def _():
        _residual_ln_inplace(o_ref, x_ref, bias_ref, g_ref, b_ref, alpha, tm)


def _proj_ln(a, w, layer, bias, x, ln_g, ln_b, ln_idx, *, alpha, tm, tk):
    m_rows, k_dim = a.shape
    d_model = x.shape[1]
    assert m_rows % tm == 0 and k_dim % tk == 0
    vec = lambda idx: pl.BlockSpec((None, 1, d_model), lambda i, k: (idx, 0, 0))
    in_specs = [pl.BlockSpec((tm, tk), lambda i, k: (i, k)),
                pl.BlockSpec((None, tk, d_model), lambda i, k: (layer, k, 0))]
    args = [a, w]
    if bias is not None:
        in_specs.append(vec(layer))
        args.append(bias.reshape(bias.shape[0], 1, -1))
    in_specs += [pl.BlockSpec((tm, d_model), lambda i, k: (i, 0)), vec(ln_idx), vec(ln_idx)]
    args += [x, ln_g.reshape(ln_g.shape[0], 1, -1), ln_b.reshape(ln_b.shape[0], 1, -1)]
    kern = functools.partial(_proj_ln_kernel, alpha=alpha, tm=tm, has_bias=bias is not None)
    return pl.pallas_call(
        kern,
        grid=(m_rows // tm, k_dim // tk),
        in_specs=in_specs,
        out_specs=pl.BlockSpec((tm, d_model), lambda i, k: (i, 0)),
        out_shape=jax.ShapeDtypeStruct((m_rows, d_model), F32),
        compiler_params=_params("arbitrary", "arbitrary"),
    )(*args)


def _ffn_kernel(x_ref, wg_ref, wu_ref, wo_ref, g_ref, b_ref, o_ref, xb_s, *, alpha, tm):
    f = pl.program_id(1)

    @pl.when(f == 0)
    def _():
        xb_s[...] = x_ref[...].astype(BF16)

    xb = xb_s[...]
    gate = jnp.dot(xb, wg_ref[...].astype(BF16), preferred_element_type=F32)
    up = jnp.dot(xb, wu_ref[...].astype(BF16), preferred_element_type=F32)
    h = (jax.nn.silu(gate) * up).astype(BF16)
    part = jnp.dot(h, wo_ref[...].astype(BF16), preferred_element_type=F32)

    @pl.when(f == 0)
    def _():
        o_ref[...] = part

    @pl.when(f > 0)
    def _():
        o_ref[...] += part

    @pl.when(f == pl.num_programs(1) - 1)
    def _():
        _residual_ln_inplace(o_ref, x_ref, None, g_ref, b_ref, alpha, tm)


def _ffn(x, w_in, w_out, layer, ln_g, ln_b, *, alpha, tm, tf):
    m_rows, d_model = x.shape
    d_ff = w_out.shape[1]
    assert m_rows % tm == 0 and d_ff % tf == 0
    nf = d_ff // tf
    vec = pl.BlockSpec((None, 1, d_model), lambda i, f: (layer, 0, 0))
    kern = functools.partial(_ffn_kernel, alpha=alpha, tm=tm)
    return pl.pallas_call(
        kern,
        grid=(m_rows // tm, nf),
        in_specs=[
            pl.BlockSpec((tm, d_model), lambda i, f: (i, 0)),
            pl.BlockSpec((None, d_model, tf), lambda i, f: (layer, 0, f)),
            pl.BlockSpec((None, d_model, tf), lambda i, f: (layer, 0, nf + f)),
            pl.BlockSpec((None, tf, d_model), lambda i, f: (layer, f, 0)),
            vec, vec,
        ],
        out_specs=pl.BlockSpec((tm, d_model), lambda i, f: (i, 0)),
        out_shape=jax.ShapeDtypeStruct((m_rows, d_model), F32),
        scratch_shapes=[pltpu.VMEM((tm, d_model), BF16)],
        compiler_params=_params("arbitrary", "arbitrary"),
    )(x, w_in, w_in, w_out, ln_g.reshape(ln_g.shape[0], 1, -1), ln_b.reshape(ln_b.shape[0], 1, -1))


def _rope_tables(pos):
    half = ROT_DIM // 2
    inv = jnp.power(ROPE_THETA, -jnp.arange(half, dtype=F32) * 2.0 / ROT_DIM)
    ang = pos.astype(F32)[:, None] * inv[None, :]
    cos, sin = jnp.cos(ang), jnp.sin(ang)
    n = pos.shape[0]
    rest = HEAD_DIM - ROT_DIM
    c = jnp.concatenate([cos, cos, jnp.ones((n, rest), F32)], axis=1)
    s_next = jnp.concatenate([-sin, jnp.zeros((n, half + rest), F32)], axis=1)
    s_prev = jnp.concatenate([jnp.zeros((n, half), F32), sin, jnp.zeros((n, rest), F32)], axis=1)
    return tuple(jnp.tile(t, (1, HEAD_W // HEAD_DIM)) for t in (c, s_next, s_prev))


def _proj_rope_kernel(*refs, tn, rope, scale, n_out):
    if rope:
        x_ref, w_ref, c_ref, sn_ref, sp_ref = refs[:5]
        out_refs = refs[5:5 + n_out]
    else:
        x_ref, w_ref = refs[:2]
        out_refs = refs[2:2 + n_out]
    xb_s = refs[-1]

    @pl.when(pl.program_id(1) == 0)
    def _():
        xb_s[...] = x_ref[...].astype(BF16)

    y = jnp.dot(xb_s[...], w_ref[...].astype(BF16), preferred_element_type=F32)
    for h in range(tn // HEAD_W):
        cols = slice(h * HEAD_W, (h + 1) * HEAD_W)
        yh = y[:, cols]
        if rope:
            half = ROT_DIM // 2
            yh = (yh * c_ref[...] + pltpu.roll(yh, HEAD_W - half, 1) * sn_ref[...]
                  + pltpu.roll(yh, half, 1) * sp_ref[...])
        if scale != 1.0:
            yh = yh * scale
        for o_ref in out_refs:
            o_ref[:, cols] = yh.astype(o_ref.dtype)


def _proj_rope(x, w, layer, col0, n_cols, tables, pos_blocks, out_dtypes, *, scale, tm, tn):
    m_rows, d_model = x.shape
    assert m_rows % tm == 0 and n_cols % tn == 0 and col0 % tn == 0 and tn % HEAD_W == 0
    j0 = col0 // tn
    rope = tables is not None
    in_specs = [pl.BlockSpec((tm, d_model), lambda i, j: (i, 0)),
                pl.BlockSpec((None, d_model, tn), lambda i, j: (layer, 0, j0 + j))]
    args = [x, w]
    if rope:
        in_specs += [pl.BlockSpec((tm, HEAD_W), lambda i, j: (i % pos_blocks, 0))] * 3
        args += list(tables)
    kern = functools.partial(_proj_rope_kernel, tn=tn, rope=rope, scale=scale, n_out=len(out_dtypes))
    return pl.pallas_call(
        kern,
        grid=(m_rows // tm, n_cols // tn),
        in_specs=in_specs,
        out_specs=[pl.BlockSpec((tm, tn), lambda i, j: (i, j)) for _ in out_dtypes],
        out_shape=[jax.ShapeDtypeStruct((m_rows, n_cols), dt) for dt in out_dtypes],
        scratch_shapes=[pltpu.VMEM((tm, d_model), BF16)],
        compiler_params=_params("arbitrary", "arbitrary"),
    )(*args)


def _diff_lambda(lam_ref, lam_init):
    lv = lam_ref[...]
    e1 = jnp.exp(jnp.sum(lv[0:1, :] * lv[1:2, :], axis=-1, keepdims=True))
    e2 = jnp.exp(jnp.sum(lv[2:3, :] * lv[3:4, :], axis=-1, keepdims=True))
    return e1 - e2 + lam_init


def _map_masks(rows):
    lane = lax.broadcasted_iota(jnp.int32, (rows, HEAD_W), 1)
    return lane < HEAD_DIM, lane >= HEAD_DIM


def _softmax_block_update(s, v, m_ref, l_ref, acc_ref):
    m_old = m_ref[...]
    m_new = jnp.maximum(m_old, jnp.max(s, axis=-1, keepdims=True))
    p = jnp.exp(s - m_new)
    corr = jnp.exp(m_old - m_new)
    l_ref[...] = corr * l_ref[...] + jnp.sum(p, axis=-1, keepdims=True)
    acc_ref[...] = corr * acc_ref[...] + jnp.dot(p.astype(BF16), v, preferred_element_type=F32)
    m_ref[...] = m_new


def _diff_finish(acc, l, n_q, lam, subln_g, lam_init):
    o = acc / l
    o = o[:n_q] - lam * o[n_q:]
    o = o * lax.rsqrt(jnp.mean(o * o, axis=-1, keepdims=True) + LN_EPS) * subln_g
    return o * (1.0 - lam_init)


def _attn_prompt_kernel(q_ref, k_ref, v_ref, lam_ref, g_ref, o_ref, q2_s, m_s, l_s, acc_s, *, tq, lam_init):
    qi = pl.program_id(2)
    q = q_ref[...]
    m0, m1 = _map_masks(tq)
    zero = jnp.zeros_like(q)
    q2_s[0:tq, :] = jnp.where(m0, q, zero)
    q2_s[tq:2 * tq, :] = jnp.where(m1, q, zero)
    m_s[...] = jnp.full_like(m_s, -jnp.inf)
    l_s[...] = jnp.zeros_like(l_s)
    acc_s[...] = jnp.zeros_like(acc_s)

    def scores(kb):
        return lax.dot_general(q2_s[...], kb, (((1,), (1,)), ((), ())), preferred_element_type=F32)

    def body(ki, carry):
        rows = pl.ds(pl.multiple_of(ki * tq, tq), tq)
        _softmax_block_update(scores(k_ref[rows, :]), v_ref[rows, :], m_s, l_s, acc_s)
        return carry

    lax.fori_loop(0, qi, body, 0)

    rows = pl.ds(pl.multiple_of(qi * tq, tq), tq)
    s = scores(k_ref[rows, :])
    r_idx = lax.broadcasted_iota(jnp.int32, (2 * tq, tq), 0)
    c_idx = lax.broadcasted_iota(jnp.int32, (2 * tq, tq), 1)
    r_idx = jnp.where(r_idx >= tq, r_idx - tq, r_idx)
    s = jnp.where(c_idx <= r_idx, s, -jnp.inf)
    _softmax_block_update(s, v_ref[rows, :], m_s, l_s, acc_s)

    lam = _diff_lambda(lam_ref, lam_init)
    o_ref[...] = _diff_finish(acc_s[...], l_s[...], tq, lam, g_ref[...], lam_init).astype(o_ref.dtype)


def _attn_prompt(qb, kb, vb, lamv, subln_g, n_batch, seq, *, lam_init, tq):
    m_rows, d_model = qb.shape
    n_heads = d_model // HEAD_W
    assert seq % tq == 0
    nq = seq // tq
    kern = functools.partial(_attn_prompt_kernel, tq=tq, lam_init=lam_init)
    return pl.pallas_call(
        kern,
        grid=(n_batch, n_heads, nq),
        in_specs=[
            pl.BlockSpec((tq, HEAD_W), lambda b, h, qi: (b * nq + qi, h)),
            pl.BlockSpec((seq, HEAD_W), lambda b, h, qi: (b, h)),
            pl.BlockSpec((seq, HEAD_W), lambda b, h, qi: (b, h)),
            pl.BlockSpec((4, HEAD_DIM), lambda b, h, qi: (0, 0)),
            pl.BlockSpec((1, HEAD_W), lambda b, h, qi: (0, 0)),
        ],
        out_specs=pl.BlockSpec((tq, HEAD_W), lambda b, h, qi: (b * nq + qi, h)),
        out_shape=jax.ShapeDtypeStruct((m_rows, d_model), BF16),
        scratch_shapes=[
            pltpu.VMEM((2 * tq, HEAD_W), BF16),
            pltpu.VMEM((2 * tq, 1), F32),
            pltpu.VMEM((2 * tq, 1), F32),
            pltpu.VMEM((2 * tq, HEAD_W), F32),
        ],
        compiler_params=_params("arbitrary", "arbitrary", "arbitrary"),
    )(qb, kb, vb, lamv, subln_g)


def _attn_sample_kernel(pt_ref, q_ref, *refs, n_heads, n_pages_step, page, dec_seq, lam_init):
    del pt_ref
    k_pages = refs[:n_pages_step]
    v_pages = refs[n_pages_step:2 * n_pages_step]
    kn_ref, vn_ref, lam_ref, g_ref, o_ref, q2_s, m_s, l_s, acc_s = refs[2 * n_pages_step:]
    j = pl.program_id(1)
    n_q = dec_seq

    @pl.when(j == 0)
    def _():
        m0, m1 = _map_masks(n_q)
        for h in range(n_heads):
            qh = q_ref[:, h * HEAD_W:(h + 1) * HEAD_W]
            q2_s[h] = jnp.concatenate([jnp.where(m0, qh, 0.0), jnp.where(m1, qh, 0.0)], axis=0).astype(BF16)
        m_s[...] = jnp.full_like(m_s, -jnp.inf)
        l_s[...] = jnp.zeros_like(l_s)
        acc_s[...] = jnp.zeros_like(acc_s)

    def head_rows(ref, h):
        return ref[pl.ds(h, page, stride=n_heads), :].astype(BF16)

    def scores(h, kh):
        return lax.dot_general(q2_s[h], kh, (((1,), (1,)), ((), ())), preferred_element_type=F32)

    for h in range(n_heads):
        kh = jnp.concatenate([head_rows(r, h) for r in k_pages], axis=0)
        vh = jnp.concatenate([head_rows(r, h) for r in v_pages], axis=0)
        _softmax_block_update(scores(h, kh), vh, m_s.at[h], l_s.at[h], acc_s.at[h])

    @pl.when(j == pl.num_programs(1) - 1)
    def _():
        lam = _diff_lambda(lam_ref, lam_init)
        r_idx = lax.broadcasted_iota(jnp.int32, (2 * n_q, page), 0)
        c_idx = lax.broadcasted_iota(jnp.int32, (2 * n_q, page), 1)
        r_idx = jnp.where(r_idx >= n_q, r_idx - n_q, r_idx)
        visible = c_idx <= r_idx
        pad = jnp.zeros((page - n_q, HEAD_W), F32)
        for h in range(n_heads):
            cols = slice(h * HEAD_W, (h + 1) * HEAD_W)
            kh = jnp.concatenate([kn_ref[:, cols], pad], axis=0).astype(BF16)
            vh = jnp.concatenate([vn_ref[:, cols], pad], axis=0).astype(BF16)
            s = jnp.where(visible, scores(h, kh), -jnp.inf)
            _softmax_block_update(s, vh, m_s.at[h], l_s.at[h], acc_s.at[h])
            o_ref[:, cols] = _diff_finish(acc_s[h], l_s[h], n_q, lam, g_ref[...], lam_init)


def _attn_sample(q, cache_k, cache_v, page_table, kn, vn, lamv, subln_g, layer, dec_seq, *,
                 lam_init, n_pages_step):
    m_rows, d_model = q.shape
    n_heads = d_model // HEAD_W
    n_dec, n_pages = page_table.shape
    page = cache_k.shape[2]
    assert n_pages % n_pages_step == 0 and dec_seq % 8 == 0 and dec_seq <= page
    n_layers, n_pool = cache_k.shape[:2]
    ck = cache_k.reshape(n_layers, n_pool, page * n_heads, HEAD_W)
    cv = cache_v.reshape(n_layers, n_pool, page * n_heads, HEAD_W)

    def page_spec(p):
        return pl.BlockSpec((None, None, page * n_heads, HEAD_W),
                            lambda b, j, pt: (layer, pt[b, j * n_pages_step + p], 0, 0))

    row_spec = pl.BlockSpec((dec_seq, d_model), lambda b, j, pt: (b, 0))
    kern = functools.partial(_attn_sample_kernel, n_heads=n_heads, n_pages_step=n_pages_step, page=page,
                             dec_seq=dec_seq, lam_init=lam_init)
    grid_spec = pltpu.PrefetchScalarGridSpec(
        num_scalar_prefetch=1,
        grid=(n_dec, n_pages // n_pages_step),
        in_specs=([row_spec] + [page_spec(p) for p in range(n_pages_step)] * 2 + [row_spec, row_spec]
                  + [pl.BlockSpec((4, HEAD_DIM), lambda b, j, pt: (0, 0)),
                     pl.BlockSpec((1, HEAD_W), lambda b, j, pt: (0, 0))]),
        out_specs=row_spec,
        scratch_shapes=[
            pltpu.VMEM((n_heads, 2 * dec_seq, HEAD_W), BF16),
            pltpu.VMEM((n_heads, 2 * dec_seq, 1), F32),
            pltpu.VMEM((n_heads, 2 * dec_seq, 1), F32),
            pltpu.VMEM((n_heads, 2 * dec_seq, HEAD_W), F32),
        ],
    )
    return pl.pallas_call(
        kern,
        grid_spec=grid_spec,
        out_shape=jax.ShapeDtypeStruct((m_rows, d_model), F32),
        compiler_params=_params("arbitrary", "arbitrary"),
    )(page_table, q, *([ck] * n_pages_step), *([cv] * n_pages_step), kn, vn, lamv, subln_g)


def _tile(n, target):
    t = min(n, target)
    assert n % t == 0
    return t


def kernel(x_prompt, x_sample, cache_k, cache_v, page_table, gm_w_in, gm_b_in, gm_ln_g, gm_ln_b, gm_w_s, gm_b_s,
           gm_w_out, gm_b_out, at_w_qkv, at_lambda_q1, at_lambda_k1, at_lambda_q2, at_lambda_k2, at_subln_g,
           at_w_out, ln_mix_g, ln_mix_b, ln_ffn_g, ln_ffn_b, ffn_w_in, ffn_w_out):
    n_batch, seq, d_model = x_prompt.shape
    n_dec, dec_seq, _ = x_sample.shape
    depth = ln_mix_g.shape[0]
    n_heads = d_model // HEAD_W
    n_pages = page_table.shape[1]
    past_len = n_pages * cache_k.shape[2]
    alpha = (2 * depth) ** 0.25
    inner = gm_w_in.shape[2] // 2
    d_ff = ffn_w_out.shape[1]

    xp = x_prompt.reshape(n_batch * seq, d_model)
    xs = x_sample.reshape(n_dec * dec_seq, d_model)
    mp, ms = xp.shape[0], xs.shape[0]

    tm_p = _tile(seq, 1024)
    tm_g = _tile(seq, 512)
    tn = _tile(inner, 512)
    tk = _tile(d_model, 256)
    tf = 256 if d_ff % 256 == 0 else d_ff

    tables_p = _rope_tables(jnp.arange(seq, dtype=jnp.int32))
    tables_s = _rope_tables(jnp.tile(past_len + jnp.arange(dec_seq, dtype=jnp.int32), n_dec))

    gm_v_p, gm_v_s, k_p, v_p, k_s, v_s = [], [], [], [], [], []
    for i in range(depth):
        j = i // 2
        if i % 2 == 0:
            bexp = jnp.repeat(gm_b_s[j].T, GROUP_W, axis=1)
            wexp = jnp.repeat(jnp.transpose(gm_w_s[j][:, :dec_seq, :dec_seq], (2, 1, 0)), GROUP_W, axis=2)
            gated_p, gv_p = _gmlp_in_prompt(xp, gm_w_in, gm_b_in, gm_ln_g, gm_ln_b, gm_w_s, bexp, j, seq,
                                            tm=tm_g, tn=tn)
            gated_s, gv_s = _gmlp_in_sample(xs, gm_w_in, gm_b_in, gm_ln_g, gm_ln_b, wexp, bexp[:dec_seq], j,
                                            dec_seq, tn=tn)
            gm_v_p.append(gv_p)
            gm_v_s.append(gv_s.reshape(n_dec, dec_seq, inner))
            mix_p = dict(a=gated_p, w=gm_w_out, bias=gm_b_out)
            mix_s = dict(a=gated_s, w=gm_w_out, bias=gm_b_out)
        else:
            lam_init = 0.8 - 0.6 * math.exp(-0.3 * i)
            lamv = jnp.stack([at_lambda_q1[j], at_lambda_k1[j], at_lambda_q2[j], at_lambda_k2[j]])
            subln = at_subln_g[j].reshape(1, HEAD_W)
            scale = HEAD_DIM ** -0.5
            proj = functools.partial(_proj_rope, w=at_w_qkv, layer=j, n_cols=d_model, tn=tn)
            (qb,) = proj(xp, col0=0, tables=tables_p, pos_blocks=seq // tm_p, out_dtypes=(BF16,),
                         scale=scale, tm=tm_p)
            kf, kb = proj(xp, col0=d_model, tables=tables_p, pos_blocks=seq // tm_p, out_dtypes=(F32, BF16),
                          scale=1.0, tm=tm_p)
            vf, vb = proj(xp, col0=2 * d_model, tables=None, pos_blocks=1, out_dtypes=(F32, BF16),
                          scale=1.0, tm=tm_p)
            a_p = _attn_prompt(qb, kb, vb, lamv, subln, n_batch, seq, lam_init=lam_init, tq=_tile(seq, 512))
            (qs,) = proj(xs, col0=0, tables=tables_s, pos_blocks=1, out_dtypes=(F32,), scale=scale, tm=ms)
            (kn,) = proj(xs, col0=d_model, tables=tables_s, pos_blocks=1, out_dtypes=(F32,), scale=1.0, tm=ms)
            (vn,) = proj(xs, col0=2 * d_model, tables=None, pos_blocks=1, out_dtypes=(F32,), scale=1.0, tm=ms)
            a_s = _attn_sample(qs, cache_k, cache_v, page_table, kn, vn, lamv, subln, j, dec_seq,
                               lam_init=lam_init, n_pages_step=4 if n_pages % 4 == 0 else 1)
            k_p.append(kf.reshape(n_batch, seq, n_heads, HEAD_W))
            v_p.append(vf.reshape(n_batch, seq, n_heads, HEAD_W))
            k_s.append(kn.reshape(n_dec, dec_seq, n_heads, HEAD_W))
            v_s.append(vn.reshape(n_dec, dec_seq, n_heads, HEAD_W))
            mix_p = dict(a=a_p, w=at_w_out, bias=None)
            mix_s = dict(a=a_s, w=at_w_out, bias=None)
        xp = _proj_ln(mix_p["a"], mix_p["w"], j, mix_p["bias"], xp, ln_mix_g, ln_mix_b, i,
                      alpha=alpha, tm=tm_p, tk=tk)
        xs = _proj_ln(mix_s["a"], mix_s["w"], j, mix_s["bias"], xs, ln_mix_g, ln_mix_b, i,
                      alpha=alpha, tm=ms, tk=tk)
        xp = _ffn(xp, ffn_w_in, ffn_w_out, i, ln_ffn_g, ln_ffn_b, alpha=alpha, tm=tm_p, tf=tf)
        xs = _ffn(xs, ffn_w_in, ffn_w_out, i, ln_ffn_g, ln_ffn_b, alpha=alpha, tm=ms, tf=tf)

    return (xp.reshape(n_batch, seq, d_model), xs.reshape(n_dec, dec_seq, d_model),
            jnp.stack(gm_v_p), jnp.stack(gm_v_s), jnp.stack(k_p), jnp.stack(v_p), jnp.stack(k_s), jnp.stack(v_s))
```

```python
import functools
import math

import jax
import jax.numpy as jnp
from jax import lax
from jax.experimental import pallas as pl
from jax.experimental.pallas import tpu as pltpu

F32 = jnp.float32
BF16 = jnp.bfloat16

LN_EPS = 1e-5
CHUNK = 128
GROUP_W = 128
HEAD_DIM = 64
HEAD_W = 2 * HEAD_DIM
ROT_DIM = HEAD_DIM // 4
ROPE_THETA = 500000.0
LANES = 128
SUBLANES = 8
MXU_COLS = 256
SOFTMAX_ROWS = 64
VMEM_LIMIT_BYTES = 60 * 1024 * 1024


def _params(*sem):
    return pltpu.CompilerParams(dimension_semantics=sem, vmem_limit_bytes=VMEM_LIMIT_BYTES)


def _layer_norm(y, g, b):
    mu = jnp.mean(y, axis=-1, keepdims=True)
    yc = y - mu
    var = jnp.mean(yc * yc, axis=-1, keepdims=True)
    return yc * lax.rsqrt(var + LN_EPS) * g + b


def _residual_ln_inplace(o_ref, x_ref, bias_ref, g_ref, b_ref, alpha, tm):
    rows_per_iter = 16

    def body(r, carry):
        rows = pl.ds(pl.multiple_of(r * rows_per_iter, rows_per_iter), rows_per_iter)
        y = alpha * x_ref[rows, :] + o_ref[rows, :]
        if bias_ref is not None:
            y = y + bias_ref[...]
        o_ref[rows, :] = _layer_norm(y, g_ref[...], b_ref[...])
        return carry

    lax.fori_loop(0, tm // rows_per_iter, body, 0)


def _gmlp_project(x_ref, w_ref, b_ref, xb_s, z_s):
    j = pl.program_id(1)

    @pl.when(j == 0)
    def _():
        xb_s[...] = x_ref[...].astype(BF16)

    z = jnp.dot(xb_s[...], w_ref[...].astype(BF16), preferred_element_type=F32) + b_ref[...]
    z_s[j] = 0.5 * z * (1.0 + lax.erf(z * math.sqrt(0.5)))


def _gmlp_v_layer_norm(z_s, lng_ref, lnb_ref, rows, n_half, tn, inner):
    vs = [z_s[n_half + c, rows, :] for c in range(n_half)]
    mu = sum(jnp.sum(v, axis=-1, keepdims=True) for v in vs) / inner
    var = sum(jnp.sum(jnp.square(v - mu), axis=-1, keepdims=True) for v in vs) / inner
    rstd = lax.rsqrt(var + LN_EPS)
    return [(vs[c] - mu) * rstd * lng_ref[:, c * tn:(c + 1) * tn] + lnb_ref[:, c * tn:(c + 1) * tn]
            for c in range(n_half)]


def _gmlp_in_prompt_kernel(x_ref, w_ref, b_ref, lng_ref, lnb_ref, ws_ref, bexp_ref,
                           gated_ref, gmv_ref, xb_s, z_s, vn_s, *, tm, tn, inner):
    _gmlp_project(x_ref, w_ref, b_ref, xb_s, z_s)
    n_half = inner // tn
    n_groups = inner // GROUP_W
    n_rc = tm // CHUNK
    per_chunk = tn // GROUP_W

    @pl.when(pl.program_id(1) == pl.num_programs(1) - 1)
    def _():
        for r in range(n_rc):
            rows = slice(r * CHUNK, (r + 1) * CHUNK)
            vn = _gmlp_v_layer_norm(z_s, lng_ref, lnb_ref, rows, n_half, tn, inner)
            for c in range(n_half):
                if r == n_rc - 1:
                    gmv_ref[:, c * tn:(c + 1) * tn] = vn[c]
                for q in range(per_chunk):
                    g = c * per_chunk + q
                    vn_s[g, :, r * CHUNK:(r + 1) * CHUNK] = vn[c][:, q * GROUP_W:(q + 1) * GROUP_W].astype(BF16)
        t_idx = lax.broadcasted_iota(jnp.int32, (CHUNK, CHUNK), 0)
        s_idx = lax.broadcasted_iota(jnp.int32, (CHUNK, CHUNK), 1)
        causal = s_idx <= t_idx
        for g in range(n_groups):
            w_causal = jnp.where(causal, ws_ref[g], 0.0).astype(BF16)
            mixed = jnp.dot(w_causal, vn_s[g], preferred_element_type=F32)
            cols = slice(g * GROUP_W, (g + 1) * GROUP_W)
            c, q = divmod(g, per_chunk)
            for r in range(n_rc):
                rows = slice(r * CHUNK, (r + 1) * CHUNK)
                u = z_s[c, rows, q * GROUP_W:(q + 1) * GROUP_W]
                m = mixed[:, r * CHUNK:(r + 1) * CHUNK] + bexp_ref[:, cols]
                gated_ref[rows, cols] = (u * m).astype(BF16)


def _gmlp_in_prompt(x, w_in, b_in, ln_g, ln_b, w_s, bexp, layer, seq, *, tm, tn):
    m_rows, d_model = x.shape
    inner = w_in.shape[2] // 2
    n_groups = inner // GROUP_W
    n_batch = m_rows // seq
    assert seq % tm == 0 and tm % CHUNK == 0 and inner % tn == 0 and tn % GROUP_W == 0
    tiles_per_seq = seq // tm
    nj = 2 * inner // tn
    kern = functools.partial(_gmlp_in_prompt_kernel, tm=tm, tn=tn, inner=inner)
    return pl.pallas_call(
        kern,
        grid=(m_rows // tm, nj),
        in_specs=[
            pl.BlockSpec((tm, d_model), lambda i, j: (i, 0)),
            pl.BlockSpec((None, d_model, tn), lambda i, j: (layer, 0, j)),
            pl.BlockSpec((None, 1, tn), lambda i, j: (layer, 0, j)),
            pl.BlockSpec((None, 1, inner), lambda i, j: (layer, 0, 0)),
            pl.BlockSpec((None, 1, inner), lambda i, j: (layer, 0, 0)),
            pl.BlockSpec((None, n_groups, CHUNK, CHUNK), lambda i, j: (layer, 0, 0, 0)),
            pl.BlockSpec((CHUNK, inner), lambda i, j: (0, 0)),
        ],
        out_specs=[
            pl.BlockSpec((tm, inner), lambda i, j: (i, 0)),
            pl.BlockSpec((None, CHUNK, inner), lambda i, j: (i // tiles_per_seq, 0, 0)),
        ],
        out_shape=[
            jax.ShapeDtypeStruct((m_rows, inner), BF16),
            jax.ShapeDtypeStruct((n_batch, CHUNK, inner), F32),
        ],
        scratch_shapes=[
            pltpu.VMEM((tm, d_model), BF16),
            pltpu.VMEM((nj, tm, tn), F32),
            pltpu.VMEM((n_groups, CHUNK, tm), BF16),
        ],
        compiler_params=_params("arbitrary", "arbitrary"),
    )(x, w_in, b_in.reshape(b_in.shape[0], 1, -1), ln_g.reshape(ln_g.shape[0], 1, -1),
      ln_b.reshape(ln_b.shape[0], 1, -1), w_s, bexp)


def _gmlp_in_sample_kernel(x_ref, w_ref, b_ref, lng_ref, lnb_ref, wexp_ref, bexp_ref,
                           gated_ref, gmv_ref, xb_s, z_s, *, tm, tn, inner, dec_seq):
    _gmlp_project(x_ref, w_ref, b_ref, xb_s, z_s)
    n_half = inner // tn

    @pl.when(pl.program_id(1) == pl.num_programs(1) - 1)
    def _():
        vn = _gmlp_v_layer_norm(z_s, lng_ref, lnb_ref, slice(0, tm), n_half, tn, inner)
        t_idx = lax.broadcasted_iota(jnp.int32, (dec_seq, tn), 0)
        for c in range(n_half):
            cols = slice(c * tn, (c + 1) * tn)
            gmv_ref[:, cols] = vn[c]
            for b in range(tm // dec_seq):
                rows = slice(b * dec_seq, (b + 1) * dec_seq)
                vb = vn[c][rows, :]
                mixed = bexp_ref[:, cols]
                for s in range(dec_seq):
                    w_ts = jnp.where(t_idx >= s, wexp_ref[s, :, cols], 0.0)
                    mixed = mixed + w_ts * vb[s:s + 1, :]
                gated_ref[rows, cols] = (z_s[c, rows, :] * mixed).astype(BF16)


def _gmlp_in_sample(x, w_in, b_in, ln_g, ln_b, wexp, bexp, layer, dec_seq, *, tn):
    m_rows, d_model = x.shape
    inner = w_in.shape[2] // 2
    assert dec_seq % 8 == 0 and dec_seq <= CHUNK and inner % tn == 0
    nj = 2 * inner // tn
    kern = functools.partial(_gmlp_in_sample_kernel, tm=m_rows, tn=tn, inner=inner, dec_seq=dec_seq)
    return pl.pallas_call(
        kern,
        grid=(1, nj),
        in_specs=[
            pl.BlockSpec((m_rows, d_model), lambda i, j: (0, 0)),
            pl.BlockSpec((None, d_model, tn), lambda i, j: (layer, 0, j)),
            pl.BlockSpec((None, 1, tn), lambda i, j: (layer, 0, j)),
            pl.BlockSpec((None, 1, inner), lambda i, j: (layer, 0, 0)),
            pl.BlockSpec((None, 1, inner), lambda i, j: (layer, 0, 0)),
            pl.BlockSpec((dec_seq, dec_seq, inner), lambda i, j: (0, 0, 0)),
            pl.BlockSpec((dec_seq, inner), lambda i, j: (0, 0)),
        ],
        out_specs=[
            pl.BlockSpec((m_rows, inner), lambda i, j: (0, 0)),
            pl.BlockSpec((m_rows, inner), lambda i, j: (0, 0)),
        ],
        out_shape=[
            jax.ShapeDtypeStruct((m_rows, inner), BF16),
            jax.ShapeDtypeStruct((m_rows, inner), F32),
        ],
        scratch_shapes=[
            pltpu.VMEM((m_rows, d_model), BF16),
            pltpu.VMEM((nj, m_rows, tn), F32),
        ],
        compiler_params=_params("arbitrary", "arbitrary"),
    )(x, w_in, b_in.reshape(b_in.shape[0], 1, -1), ln_g.reshape(ln_g.shape[0], 1, -1),
      ln_b.reshape(ln_b.shape[0], 1, -1), wexp, bexp)


def _proj_ln_kernel(*refs, alpha, tm, has_bias):
    if has_bias:
        a_ref, w_ref, bias_ref, x_ref, g_ref, b_ref, o_ref = refs
    else:
        a_ref, w_ref, x_ref, g_ref, b_ref, o_ref = refs
        bias_ref = None
    k = pl.program_id(1)
    part = jnp.dot(a_ref[...].astype(BF16), w_ref[...].astype(BF16), preferred_element_type=F32)

    @pl.when(k == 0)
    def _():
        o_ref[...] = part

    @pl.when(k > 0)
    def _():
        o_ref[...] += part

    @pl.when(k == pl.num_programs(1) - 1)
    def _():
        _residual_ln_inplace(o_ref, x_ref, bias_ref, g_ref, b_ref, alpha, tm)


def _proj_ln(a, w, layer, bias, x, ln_g, ln_b, ln_idx, *, alpha, tm, tk):
    m_rows, k_dim = a.shape
    d_model = x.shape[1]
    assert m_rows % tm == 0 and k_dim % tk == 0
    vec = lambda idx: pl.BlockSpec((None, 1, d_model), lambda i, k: (idx, 0, 0))
    in_specs = [pl.BlockSpec((tm, tk), lambda i, k: (i, k)),
                pl.BlockSpec((None, tk, d_model), lambda i, k: (layer, k, 0))]
    args = [a, w]
    if bias is not None:
        in_specs.append(vec(layer))
        args.append(bias.reshape(bias.shape[0], 1, -1))
    in_specs += [pl.BlockSpec((tm, d_model), lambda i, k: (i, 0)), vec(ln_idx), vec(ln_idx)]
    args += [x, ln_g.reshape(ln_g.shape[0], 1, -1), ln_b.reshape(ln_b.shape[0], 1, -1)]
    kern = functools.partial(_proj_ln_kernel, alpha=alpha, tm=tm, has_bias=bias is not None)
    return pl.pallas_call(
        kern,
        grid=(m_rows // tm, k_dim // tk),
        in_specs=in_specs,
        out_specs=pl.BlockSpec((tm, d_model), lambda i, k: (i, 0)),
        out_shape=jax.ShapeDtypeStruct((m_rows, d_model), F32),
        compiler_params=_params("arbitrary", "arbitrary"),
    )(*args)


def _ffn_kernel(x_ref, wg_ref, wu_ref, wo_ref, g_ref, b_ref, o_ref, xb_s, *, alpha, tm):
    f = pl.program_id(1)

    @pl.when(f == 0)
    def _():
        xb_s[...] = x_ref[...].astype(BF16)

    xb = xb_s[...]
    gate = jnp.dot(xb, wg_ref[...].astype(BF16), preferred_element_type=F32)
    up = jnp.dot(xb, wu_ref[...].astype(BF16), preferred_element_type=F32)
    h = (jax.nn.silu(gate) * up).astype(BF16)
    part = jnp.dot(h, wo_ref[...].astype(BF16), preferred_element_type=F32)

    @pl.when(f == 0)
    def _():
        o_ref[...] = part

    @pl.when(f > 0)
    def _():
        o_ref[...] += part

    @pl.when(f == pl.num_programs(1) - 1)
    def _():
        _residual_ln_inplace(o_ref, x_ref, None, g_ref, b_ref, alpha, tm)


def _ffn(x, w_in, w_out, layer, ln_g, ln_b, *, alpha, tm, tf):
    m_rows, d_model = x.shape
    d_ff = w_out.shape[1]
    assert m_rows % tm == 0 and d_ff % tf == 0
    nf = d_ff // tf
    vec = pl.BlockSpec((None, 1, d_model), lambda i, f: (layer, 0, 0))
    kern = functools.partial(_ffn_kernel, alpha=alpha, tm=tm)
    return pl.pallas_call(
        kern,
        grid=(m_rows // tm, nf),
        in_specs=[
            pl.BlockSpec((tm, d_model), lambda i, f: (i, 0)),
            pl.BlockSpec((None, d_model, tf), lambda i, f: (layer, 0, f)),
            pl.BlockSpec((None, d_model, tf), lambda i, f: (layer, 0, nf + f)),
            pl.BlockSpec((None, tf, d_model), lambda i, f: (layer, f, 0)),
            vec, vec,
        ],
        out_specs=pl.BlockSpec((tm, d_model), lambda i, f: (i, 0)),
        out_shape=jax.ShapeDtypeStruct((m_rows, d_model), F32),
        scratch_shapes=[pltpu.VMEM((tm, d_model), BF16)],
        compiler_params=_params("arbitrary", "arbitrary"),
    )(x, w_in, w_in, w_out, ln_g.reshape(ln_g.shape[0], 1, -1), ln_b.reshape(ln_b.shape[0], 1, -1))


def _rope_tables(pos):
    half = ROT_DIM // 2
    inv = jnp.power(ROPE_THETA, -jnp.arange(half, dtype=F32) * 2.0 / ROT_DIM)
    ang = pos.astype(F32)[:, None] * inv[None, :]
    cos, sin = jnp.cos(ang), jnp.sin(ang)
    n = pos.shape[0]
    rest = HEAD_DIM - ROT_DIM
    c = jnp.concatenate([cos, cos, jnp.ones((n, rest), F32)], axis=1)
    s_next = jnp.concatenate([-sin, jnp.zeros((n, half + rest), F32)], axis=1)
    s_prev = jnp.concatenate([jnp.zeros((n, half), F32), sin, jnp.zeros((n, rest), F32)], axis=1)
    return tuple(jnp.tile(t, (1, HEAD_W // HEAD_DIM)) for t in (c, s_next, s_prev))


def _proj_rope_kernel(*refs, tn, rope, scale, n_out, vt_tk):
    n_in = 5 if rope else 2
    x_ref, w_ref = refs[:2]
    out_refs = refs[n_in:n_in + n_out]
    vt_ref = refs[n_in + n_out] if vt_tk else None
    xb_s = refs[-1]

    @pl.when(pl.program_id(1) == 0)
    def _():
        xb_s[...] = x_ref[...].astype(BF16)

    y = jnp.dot(xb_s[...], w_ref[...].astype(BF16), preferred_element_type=F32)
    for h in range(tn // HEAD_W):
        cols = slice(h * HEAD_W, (h + 1) * HEAD_W)
        yh = y[:, cols]
        if rope:
            c_ref, sn_ref, sp_ref = refs[2:5]
            half = ROT_DIM // 2
            yh = (yh * c_ref[...] + pltpu.roll(yh, HEAD_W - half, 1) * sn_ref[...]
                  + pltpu.roll(yh, half, 1) * sp_ref[...])
        if scale != 1.0:
            yh = yh * scale
        for o_ref in out_refs:
            o_ref[:, cols] = yh.astype(o_ref.dtype)
        if vt_ref is not None:
            for kb in range(y.shape[0] // vt_tk):
                vt_ref[h, kb] = yh[kb * vt_tk:(kb + 1) * vt_tk, :].T.astype(BF16)


def _proj_rope(x, w, layer, col0, n_cols, tables, pos_blocks, out_dtypes, *, scale, tm, tn, vt=None):
    m_rows, d_model = x.shape
    assert m_rows % tm == 0 and n_cols % tn == 0 and col0 % tn == 0 and tn % HEAD_W == 0
    j0 = col0 // tn
    rope = tables is not None
    in_specs = [pl.BlockSpec((tm, d_model), lambda i, j: (i, 0)),
                pl.BlockSpec((None, d_model, tn), lambda i, j: (layer, 0, j0 + j))]
    args = [x, w]
    if rope:
        in_specs += [pl.BlockSpec((tm, HEAD_W), lambda i, j: (i % pos_blocks, 0))] * 3
        args += list(tables)
    out_specs = [pl.BlockSpec((tm, tn), lambda i, j: (i, j)) for _ in out_dtypes]
    out_shape = [jax.ShapeDtypeStruct((m_rows, n_cols), dt) for dt in out_dtypes]
    vt_tk = 0
    if vt is not None:
        seq, vt_tk = vt
        assert seq % tm == 0 and tm % vt_tk == 0
        tiles_per_seq, head_blocks, heads_per_block = seq // tm, n_cols // tn, tn // HEAD_W
        out_specs.append(pl.BlockSpec(
            (heads_per_block, tm // vt_tk, HEAD_W, vt_tk),
            lambda i, j: ((i // tiles_per_seq) * head_blocks + j, i % tiles_per_seq, 0, 0)))
        out_shape.append(jax.ShapeDtypeStruct(
            ((m_rows // seq) * (n_cols // HEAD_W), seq // vt_tk, HEAD_W, vt_tk), BF16))
    kern = functools.partial(_proj_rope_kernel, tn=tn, rope=rope, scale=scale, n_out=len(out_dtypes), vt_tk=vt_tk)
    return pl.pallas_call(
        kern,
        grid=(m_rows // tm, n_cols // tn),
        in_specs=in_specs,
        out_specs=out_specs,
        out_shape=out_shape,
        scratch_shapes=[pltpu.VMEM((tm, d_model), BF16)],
        compiler_params=_params("arbitrary", "arbitrary"),
    )(*args)


def _diff_lambda(lam_ref, lam_init):
    lv = lam_ref[...]
    e1 = jnp.exp(jnp.sum(lv[0:1, :] * lv[1:2, :], axis=-1, keepdims=True))
    e2 = jnp.exp(jnp.sum(lv[2:3, :] * lv[3:4, :], axis=-1, keepdims=True))
    return e1 - e2 + lam_init


def _map_masks(rows):
    lane = lax.broadcasted_iota(jnp.int32, (rows, HEAD_W), 1)
    return lane < HEAD_DIM, lane >= HEAD_DIM


def _sub_ln(o, g_col, lam_init):
    return o * lax.rsqrt(jnp.mean(o * o, axis=0, keepdims=True) + LN_EPS) * g_col * (1.0 - lam_init)


def _attn_prompt_kernel(q_ref, k_ref, vt_ref, lam_ref, g_ref, o_ref, q2_s, m_s, l_s, acc_s, c_s, mx_s, s_s, p_s,
                        *, tq, lam_init):
    qi = pl.program_id(2)
    q = q_ref[...]
    m0, m1 = _map_masks(tq)
    zero = jnp.zeros_like(q)
    q2_s[0:tq, :] = jnp.where(m0, q, zero)
    q2_s[tq:2 * tq, :] = jnp.where(m1, q, zero)
    m_s[...] = jnp.full_like(m_s, -jnp.inf)
    l_s[...] = jnp.zeros_like(l_s)
    acc_s[...] = jnp.zeros_like(acc_s)
    gw = min(MXU_COLS, tq)

    def block(ki, diagonal):
        kb = k_ref[pl.ds(pl.multiple_of(ki * tq, tq), tq), :]
        vtb = vt_ref[ki]
        n_g = 2 * tq // gw
        for g in range(n_g):
            cols = slice(g * gw, (g + 1) * gw)
            s = lax.dot_general(kb, q2_s[cols, :], (((1,), (1,)), ((), ())), preferred_element_type=F32)
            if diagonal:
                key = lax.broadcasted_iota(jnp.int32, (tq, gw), 0)
                qry = lax.broadcasted_iota(jnp.int32, (tq, gw), 1) + (g * gw) % tq
                s = jnp.where(key <= qry, s, -jnp.inf)
            s_s[g] = s
            mx_s[:, cols] = jnp.max(s, axis=0, keepdims=True)
        for g in range(n_g):
            cols = slice(g * gw, (g + 1) * gw)
            m_old = m_s[:, cols]
            m_new = jnp.maximum(m_old, mx_s[:, cols])
            c_s[:, cols] = jnp.exp(m_old - m_new)
            p_sum = jnp.zeros((1, gw), F32)
            for r in range(0, tq, SOFTMAX_ROWS):
                p = jnp.exp(s_s[g, r:r + SOFTMAX_ROWS, :] - m_new)
                p_sum = p_sum + jnp.sum(p, axis=0, keepdims=True)
                p_s[g, r:r + SOFTMAX_ROWS, :] = p.astype(BF16)
            l_s[:, cols] = c_s[:, cols] * l_s[:, cols] + p_sum
            m_s[:, cols] = m_new
        for g in range(n_g):
            cols = slice(g * gw, (g + 1) * gw)
            acc_s[:, cols] = c_s[:, cols] * acc_s[:, cols] + jnp.dot(vtb, p_s[g], preferred_element_type=F32)

    def body(ki, carry):
        block(ki, False)
        return carry

    lax.fori_loop(0, qi, body, 0)
    block(qi, True)

    o = acc_s[...] / l_s[...]
    o = o[:, :tq] - _diff_lambda(lam_ref, lam_init) * o[:, tq:]
    o_ref[...] = _sub_ln(o, g_ref[...], lam_init).T.astype(o_ref.dtype)


def _attn_prompt(qb, kb, vt, lamv, subln_col, n_batch, seq, *, lam_init, tq):
    m_rows, d_model = qb.shape
    n_heads = d_model // HEAD_W
    assert seq % tq == 0 and vt.shape[1:] == (seq // tq, HEAD_W, tq)
    nq = seq // tq
    gw = min(MXU_COLS, tq)
    kern = functools.partial(_attn_prompt_kernel, tq=tq, lam_init=lam_init)
    return pl.pallas_call(
        kern,
        grid=(n_batch, n_heads, nq),
        in_specs=[
            pl.BlockSpec((tq, HEAD_W), lambda b, h, qi: (b * nq + qi, h)),
            pl.BlockSpec((seq, HEAD_W), lambda b, h, qi: (b, h)),
            pl.BlockSpec((None, nq, HEAD_W, tq), lambda b, h, qi: (b * n_heads + h, 0, 0, 0)),
            pl.BlockSpec((4, HEAD_DIM), lambda b, h, qi: (0, 0)),
            pl.BlockSpec((HEAD_W, 1), lambda b, h, qi: (0, 0)),
        ],
        out_specs=pl.BlockSpec((tq, HEAD_W), lambda b, h, qi: (b * nq + qi, h)),
        out_shape=jax.ShapeDtypeStruct((m_rows, d_model), BF16),
        scratch_shapes=[
            pltpu.VMEM((2 * tq, HEAD_W), BF16),
            pltpu.VMEM((1, 2 * tq), F32),
            pltpu.VMEM((1, 2 * tq), F32),
            pltpu.VMEM((HEAD_W, 2 * tq), F32),
            pltpu.VMEM((1, 2 * tq), F32),
            pltpu.VMEM((1, 2 * tq), F32),
            pltpu.VMEM((2 * tq // gw, tq, gw), F32),
            pltpu.VMEM((2 * tq // gw, tq, gw), BF16),
        ],
        compiler_params=_params("arbitrary", "arbitrary", "arbitrary"),
    )(qb, kb, vt, lamv, subln_col)


def _attn_sample_kernel(pt_ref, q_ref, *refs, n_heads, n_slots, dec_seq, lam_init):
    del pt_ref
    k_pages = refs[:n_slots]
    v_pages = refs[n_slots:2 * n_slots]
    kn_ref, vn_ref, lam_ref, g_ref, o_ref, qm_s, m_s, l_s, acc_s = refs[2 * n_slots:]
    j = pl.program_id(1)
    hp = min(n_heads, SUBLANES)
    n_groups = n_heads // hp
    cw = 2 * dec_seq
    width = hp * cw

    @pl.when(j == 0)
    def _():
        m0, m1 = _map_masks(dec_seq)
        for h in range(n_heads):
            grp, hl = divmod(h, hp)
            qh = q_ref[:, h * HEAD_W:(h + 1) * HEAD_W]
            qm_s[grp, hl * cw:(hl + 1) * cw, :] = jnp.concatenate(
                [jnp.where(m0, qh, 0.0), jnp.where(m1, qh, 0.0)], axis=0).astype(BF16)
        m_s[...] = jnp.full_like(m_s, -jnp.inf)
        l_s[...] = jnp.zeros_like(l_s)
        acc_s[...] = jnp.zeros_like(acc_s)

    own = (lax.broadcasted_iota(jnp.int32, (hp, width), 1) // cw
           == lax.broadcasted_iota(jnp.int32, (hp, width), 0))

    def update(slot, k3_ref, v3_ref, causal):
        n_keys = k3_ref.shape[0]
        for grp in range(n_groups):
            heads = slice(grp * hp, (grp + 1) * hp)
            kr = k3_ref[:, heads, :].reshape(n_keys * hp, HEAD_W).astype(BF16)
            u = lax.dot_general(kr, qm_s[grp], (((1,), (1,)), ((), ())), preferred_element_type=F32)
            u = u.reshape(n_keys, hp, width)
            keep = own
            if causal:
                key = lax.broadcasted_iota(jnp.int32, (n_keys, hp, width), 0)
                qry = lax.broadcasted_iota(jnp.int32, (n_keys, hp, width), 2) % dec_seq
                keep = jnp.logical_and(own, key <= qry)
                u = jnp.where(key <= qry, u, -jnp.inf)
            m_old = m_s[slot, grp]
            m_new = jnp.maximum(m_old, jnp.max(u, axis=0))
            p = jnp.where(keep, jnp.exp(u - m_new), 0.0)
            corr = jnp.exp(m_old - m_new)
            l_s[slot, grp] = corr * l_s[slot, grp] + jnp.sum(p, axis=0)
            m_s[slot, grp] = m_new
            vr = v3_ref[:, heads, :].reshape(n_keys * hp, HEAD_W).astype(BF16)
            pv = lax.dot_general(vr, p.reshape(n_keys * hp, width).astype(BF16), (((0,), (0,)), ((), ())),
                                 preferred_element_type=F32)
            corr_col = jnp.sum(jnp.where(own, corr, 0.0), axis=0, keepdims=True)
            acc_s[slot, grp] = acc_s[slot, grp] * corr_col + pv

    for slot in range(n_slots):
        update(slot, k_pages[slot], v_pages[slot], False)

    @pl.when(j == pl.num_programs(1) - 1)
    def _():
        update(0, kn_ref, vn_ref, True)
        lam = _diff_lambda(lam_ref, lam_init)
        for grp in range(n_groups):
            m_all = m_s[0, grp]
            for slot in range(1, n_slots):
                m_all = jnp.maximum(m_all, m_s[slot, grp])
            l_all = jnp.zeros((hp, width), F32)
            acc = jnp.zeros((HEAD_W, width), F32)
            for slot in range(n_slots):
                w = jnp.exp(m_s[slot, grp] - m_all)
                l_all = l_all + w * l_s[slot, grp]
                acc = acc + acc_s[slot, grp] * jnp.sum(jnp.where(own, w, 0.0), axis=0, keepdims=True)
            o = acc / jnp.sum(jnp.where(own, l_all, 0.0), axis=0, keepdims=True)
            o = o - lam * pltpu.roll(o, width - dec_seq, 1)
            ot = _sub_ln(o, g_ref[...], lam_init).T
            for hl in range(hp):
                h = grp * hp + hl
                o_ref[:, h * HEAD_W:(h + 1) * HEAD_W] = ot[hl * cw:hl * cw + dec_seq, :]


def _attn_sample(q, cache_k, cache_v, page_table, kn, vn, lamv, subln_col, layer, dec_seq, *,
                 lam_init, n_slots):
    m_rows, d_model = q.shape
    n_heads = d_model // HEAD_W
    n_dec, n_pages = page_table.shape
    page = cache_k.shape[2]
    hp = min(n_heads, SUBLANES)
    assert n_pages % n_slots == 0 and dec_seq % SUBLANES == 0 and n_heads % hp == 0
    width = hp * 2 * dec_seq
    n_groups = n_heads // hp

    def page_spec(slot):
        return pl.BlockSpec((None, None, page, n_heads, HEAD_W),
                            lambda b, j, pt: (layer, pt[b, j * n_slots + slot], 0, 0, 0))

    row_spec = pl.BlockSpec((dec_seq, d_model), lambda b, j, pt: (b, 0))
    new_spec = pl.BlockSpec((None, dec_seq, n_heads, HEAD_W), lambda b, j, pt: (b, 0, 0, 0))
    kern = functools.partial(_attn_sample_kernel, n_heads=n_heads, n_slots=n_slots, dec_seq=dec_seq,
                             lam_init=lam_init)
    grid_spec = pltpu.PrefetchScalarGridSpec(
        num_scalar_prefetch=1,
        grid=(n_dec, n_pages // n_slots),
        in_specs=([row_spec] + [page_spec(s) for s in range(n_slots)] * 2 + [new_spec, new_spec]
                  + [pl.BlockSpec((4, HEAD_DIM), lambda b, j, pt: (0, 0)),
                     pl.BlockSpec((HEAD_W, 1), lambda b, j, pt: (0, 0))]),
        out_specs=row_spec,
        scratch_shapes=[
            pltpu.VMEM((n_groups, width, HEAD_W), BF16),
            pltpu.VMEM((n_slots, n_groups, hp, width), F32),
            pltpu.VMEM((n_slots, n_groups, hp, width), F32),
            pltpu.VMEM((n_slots, n_groups, HEAD_W, width), F32),
        ],
    )
    new_rows = lambda a: a.reshape(n_dec, dec_seq, n_heads, HEAD_W)
    return pl.pallas_call(
        kern,
        grid_spec=grid_spec,
        out_shape=jax.ShapeDtypeStruct((m_rows, d_model), F32),
        compiler_params=_params("arbitrary", "arbitrary"),
    )(page_table, q, *([cache_k] * n_slots), *([cache_v] * n_slots), new_rows(kn), new_rows(vn), lamv, subln_col)


def _tile(n, target):
    t = min(n, target)
    assert n % t == 0
    return t


def kernel(x_prompt, x_sample, cache_k, cache_v, page_table, gm_w_in, gm_b_in, gm_ln_g, gm_ln_b, gm_w_s, gm_b_s,
           gm_w_out, gm_b_out, at_w_qkv, at_lambda_q1, at_lambda_k1, at_lambda_q2, at_lambda_k2, at_subln_g,
           at_w_out, ln_mix_g, ln_mix_b, ln_ffn_g, ln_ffn_b, ffn_w_in, ffn_w_out):
    n_batch, seq, d_model = x_prompt.shape
    n_dec, dec_seq, _ = x_sample.shape
    depth = ln_mix_g.shape[0]
    n_heads = d_model // HEAD_W
    n_pages = page_table.shape[1]
    past_len = n_pages * cache_k.shape[2]
    alpha = (2 * depth) ** 0.25
    inner = gm_w_in.shape[2] // 2
    d_ff = ffn_w_out.shape[1]

    xp = x_prompt.reshape(n_batch * seq, d_model)
    xs = x_sample.reshape(n_dec * dec_seq, d_model)
    mp, ms = xp.shape[0], xs.shape[0]

    tm_p = _tile(seq, 1024)
    tm_g = _tile(seq, 512)
    tn = _tile(inner, 512)
    tk = _tile(d_model, 512)
    tf = 256 if d_ff % 256 == 0 else d_ff

    tables_p = _rope_tables(jnp.arange(seq, dtype=jnp.int32))
    tables_s = _rope_tables(jnp.tile(past_len + jnp.arange(dec_seq, dtype=jnp.int32), n_dec))

    gm_v_p, gm_v_s, k_p, v_p, k_s, v_s = [], [], [], [], [], []
    for i in range(depth):
        j = i // 2
        if i % 2 == 0:
            bexp = jnp.repeat(gm_b_s[j].T, GROUP_W, axis=1)
            wexp = jnp.repeat(jnp.transpose(gm_w_s[j][:, :dec_seq, :dec_seq], (2, 1, 0)), GROUP_W, axis=2)
            gated_p, gv_p = _gmlp_in_prompt(xp, gm_w_in, gm_b_in, gm_ln_g, gm_ln_b, gm_w_s, bexp, j, seq,
                                            tm=tm_g, tn=tn)
            gated_s, gv_s = _gmlp_in_sample(xs, gm_w_in, gm_b_in, gm_ln_g, gm_ln_b, wexp, bexp[:dec_seq], j,
                                            dec_seq, tn=tn)
            gm_v_p.append(gv_p)
            gm_v_s.append(gv_s.reshape(n_dec, dec_seq, inner))
            mix_p = dict(a=gated_p, w=gm_w_out, bias=gm_b_out)
            mix_s = dict(a=gated_s, w=gm_w_out, bias=gm_b_out)
        else:
            lam_init = 0.8 - 0.6 * math.exp(-0.3 * i)
            lamv = jnp.stack([at_lambda_q1[j], at_lambda_k1[j], at_lambda_q2[j], at_lambda_k2[j]])
            subln = at_subln_g[j].reshape(HEAD_W, 1)
            scale = HEAD_DIM ** -0.5
            tq = _tile(seq, 512)
            proj = functools.partial(_proj_rope, w=at_w_qkv, layer=j, n_cols=d_model, tn=tn)
            (qb,) = proj(xp, col0=0, tables=tables_p, pos_blocks=seq // tm_p, out_dtypes=(BF16,),
                         scale=scale, tm=tm_p)
            kf, kb = proj(xp, col0=d_model, tables=tables_p, pos_blocks=seq // tm_p, out_dtypes=(F32, BF16),
                          scale=1.0, tm=tm_p)
            vf, vt = proj(xp, col0=2 * d_model, tables=None, pos_blocks=1, out_dtypes=(F32,),
                          scale=1.0, tm=tm_p, vt=(seq, tq))
            a_p = _attn_prompt(qb, kb, vt, lamv, subln, n_batch, seq, lam_init=lam_init, tq=tq)
            (qs,) = proj(xs, col0=0, tables=tables_s, pos_blocks=1, out_dtypes=(F32,), scale=scale, tm=ms)
            (kn,) = proj(xs, col0=d_model, tables=tables_s, pos_blocks=1, out_dtypes=(F32,), scale=1.0, tm=ms)
            (vn,) = proj(xs, col0=2 * d_model, tables=None, pos_blocks=1, out_dtypes=(F32,), scale=1.0, tm=ms)
            a_s = _attn_sample(qs, cache_k, cache_v, page_table, kn, vn, lamv, subln, j, dec_seq,
                               lam_init=lam_init, n_slots=4 if n_pages % 4 == 0 else 1)
            k_p.append(kf.reshape(n_batch, seq, n_heads, HEAD_W))
            v_p.append(vf.reshape(n_batch, seq, n_heads, HEAD_W))
            k_s.append(kn.reshape(n_dec, dec_seq, n_heads, HEAD_W))
            v_s.append(vn.reshape(n_dec, dec_seq, n_heads, HEAD_W))
            mix_p = dict(a=a_p, w=at_w_out, bias=None)
            mix_s = dict(a=a_s, w=at_w_out, bias=None)
        xp = _proj_ln(mix_p["a"], mix_p["w"], j, mix_p["bias"], xp, ln_mix_g, ln_mix_b, i,
                      alpha=alpha, tm=tm_p, tk=tk)
        xs = _proj_ln(mix_s["a"], mix_s["w"], j, mix_s["bias"], xs, ln_mix_g, ln_mix_b, i,
                      alpha=alpha, tm=ms, tk=tk)
        xp = _ffn(xp, ffn_w_in, ffn_w_out, i, ln_ffn_g, ln_ffn_b, alpha=alpha, tm=tm_p, tf=tf)
        xs = _ffn(xs, ffn_w_in, ffn_w_out, i, ln_ffn_g, ln_ffn_b, alpha=alpha, tm=ms, tf=tf)

    return (xp.reshape(n_batch, seq, d_model), xs.reshape(n_dec, dec_seq, d_model),
            jnp.stack(gm_v_p), jnp.stack(gm_v_s), jnp.stack(k_p), jnp.stack(v_p), jnp.stack(k_s), jnp.stack(v_s))
```

```python
import functools
import math

import jax
import jax.numpy as jnp
from jax import lax
from jax.experimental import pallas as pl
from jax.experimental.pallas import tpu as pltpu

F32 = jnp.float32
BF16 = jnp.bfloat16

LN_EPS = 1e-5
CHUNK = 128
GROUP_W = 128
HEAD_DIM = 64
HEAD_W = 2 * HEAD_DIM
ROT_DIM = HEAD_DIM // 4
ROPE_THETA = 500000.0
LANES = 128
SUBLANES = 8
MXU_COLS = 256
SOFTMAX_ROWS = 64
VMEM_LIMIT_BYTES = 60 * 1024 * 1024


def _params(*sem):
    return pltpu.CompilerParams(dimension_semantics=sem, vmem_limit_bytes=VMEM_LIMIT_BYTES)


def _layer_norm(y, g, b):
    mu = jnp.mean(y, axis=-1, keepdims=True)
    yc = y - mu
    var = jnp.mean(yc * yc, axis=-1, keepdims=True)
    return yc * lax.rsqrt(var + LN_EPS) * g + b


def _residual_ln_inplace(o_ref, x_ref, bias_ref, g_ref, b_ref, alpha, tm):
    sub, n_sub = SUBLANES, 4
    assert tm % (sub * n_sub) == 0

    def body(r, carry):
        base = pl.multiple_of(r * (sub * n_sub), sub * n_sub)
        ys = []
        for k in range(n_sub):
            rows = pl.ds(base + k * sub, sub)
            y = alpha * x_ref[rows, :] + o_ref[rows, :]
            ys.append(y if bias_ref is None else y + bias_ref[...])
        outs = [_layer_norm(y, g_ref[...], b_ref[...]) for y in ys]
        for k in range(n_sub):
            o_ref[pl.ds(base + k * sub, sub), :] = outs[k]
        return carry

    lax.fori_loop(0, tm // (sub * n_sub), body, 0)


def _cast_rows_to_bf16(dst_ref, src_ref):
    rows = 256
    n = src_ref.shape[0]
    assert n % rows == 0

    def body(r, carry):
        sl = pl.ds(pl.multiple_of(r * rows, rows), rows)
        dst_ref[sl, :] = src_ref[sl, :].astype(BF16)
        return carry

    lax.fori_loop(0, n // rows, body, 0)


def _gmlp_project(x_ref, w_ref, b_ref, xb_s, z_s):
    j = pl.program_id(1)

    @pl.when(j == 0)
    def _():
        xb_s[...] = x_ref[...].astype(BF16)

    z = jnp.dot(xb_s[...], w_ref[...].astype(BF16), preferred_element_type=F32) + b_ref[...]
    z_s[j] = 0.5 * z * (1.0 + lax.erf(z * math.sqrt(0.5)))


def _gmlp_v_layer_norm(z_s, lng_ref, lnb_ref, rows, n_half, tn, inner):
    vs = [z_s[n_half + c, rows, :] for c in range(n_half)]
    mu = sum(jnp.sum(v, axis=-1, keepdims=True) for v in vs) / inner
    var = sum(jnp.sum(jnp.square(v - mu), axis=-1, keepdims=True) for v in vs) / inner
    rstd = lax.rsqrt(var + LN_EPS)
    return [(vs[c] - mu) * rstd * lng_ref[:, c * tn:(c + 1) * tn] + lnb_ref[:, c * tn:(c + 1) * tn]
            for c in range(n_half)]


def _gmlp_in_prompt_kernel(x_ref, w_ref, b_ref, lng_ref, lnb_ref, ws_ref, bexp_ref,
                           gated_ref, gmv_ref, xb_s, z_s, vn_s, *, tm, tn, inner):
    _gmlp_project(x_ref, w_ref, b_ref, xb_s, z_s)
    n_half = inner // tn
    n_groups = inner // GROUP_W
    n_rc = tm // CHUNK
    per_chunk = tn // GROUP_W

    @pl.when(pl.program_id(1) == pl.num_programs(1) - 1)
    def _():
        for r in range(n_rc):
            rows = slice(r * CHUNK, (r + 1) * CHUNK)
            vn = _gmlp_v_layer_norm(z_s, lng_ref, lnb_ref, rows, n_half, tn, inner)
            for c in range(n_half):
                if r == n_rc - 1:
                    gmv_ref[:, c * tn:(c + 1) * tn] = vn[c]
                for q in range(per_chunk):
                    g = c * per_chunk + q
                    vn_s[g, :, r * CHUNK:(r + 1) * CHUNK] = vn[c][:, q * GROUP_W:(q + 1) * GROUP_W].astype(BF16)
        t_idx = lax.broadcasted_iota(jnp.int32, (CHUNK, CHUNK), 0)
        s_idx = lax.broadcasted_iota(jnp.int32, (CHUNK, CHUNK), 1)
        causal = s_idx <= t_idx
        for g in range(n_groups):
            w_causal = jnp.where(causal, ws_ref[g], 0.0).astype(BF16)
            mixed = jnp.dot(w_causal, vn_s[g], preferred_element_type=F32)
            cols = slice(g * GROUP_W, (g + 1) * GROUP_W)
            c, q = divmod(g, per_chunk)
            for r in range(n_rc):
                rows = slice(r * CHUNK, (r + 1) * CHUNK)
                u = z_s[c, rows, q * GROUP_W:(q + 1) * GROUP_W]
                m = mixed[:, r * CHUNK:(r + 1) * CHUNK] + bexp_ref[:, cols]
                gated_ref[rows, cols] = (u * m).astype(BF16)


def _gmlp_in_prompt(x, w_in, b_in, ln_g, ln_b, w_s, bexp, layer, seq, *, tm, tn):
    m_rows, d_model = x.shape
    inner = w_in.shape[2] // 2
    n_groups = inner // GROUP_W
    n_batch = m_rows // seq
    assert seq % tm == 0 and tm % CHUNK == 0 and inner % tn == 0 and tn % GROUP_W == 0
    tiles_per_seq = seq // tm
    nj = 2 * inner // tn
    kern = functools.partial(_gmlp_in_prompt_kernel, tm=tm, tn=tn, inner=inner)
    return pl.pallas_call(
        kern,
        grid=(m_rows // tm, nj),
        in_specs=[
            pl.BlockSpec((tm, d_model), lambda i, j: (i, 0)),
            pl.BlockSpec((None, d_model, tn), lambda i, j: (layer, 0, j)),
            pl.BlockSpec((None, 1, tn), lambda i, j: (layer, 0, j)),
            pl.BlockSpec((None, 1, inner), lambda i, j: (layer, 0, 0)),
            pl.BlockSpec((None, 1, inner), lambda i, j: (layer, 0, 0)),
            pl.BlockSpec((None, n_groups, CHUNK, CHUNK), lambda i, j: (layer, 0, 0, 0)),
            pl.BlockSpec((CHUNK, inner), lambda i, j: (0, 0)),
        ],
        out_specs=[
            pl.BlockSpec((tm, inner), lambda i, j: (i, 0)),
            pl.BlockSpec((None, CHUNK, inner), lambda i, j: (i // tiles_per_seq, 0, 0)),
        ],
        out_shape=[
            jax.ShapeDtypeStruct((m_rows, inner), BF16),
            jax.ShapeDtypeStruct((n_batch, CHUNK, inner), F32),
        ],
        scratch_shapes=[
            pltpu.VMEM((tm, d_model), BF16),
            pltpu.VMEM((nj, tm, tn), F32),
            pltpu.VMEM((n_groups, CHUNK, tm), BF16),
        ],
        compiler_params=_params("arbitrary", "arbitrary"),
    )(x, w_in, b_in.reshape(b_in.shape[0], 1, -1), ln_g.reshape(ln_g.shape[0], 1, -1),
      ln_b.reshape(ln_b.shape[0], 1, -1), w_s, bexp)


def _gmlp_in_sample_kernel(x_ref, w_ref, b_ref, lng_ref, lnb_ref, wexp_ref, bexp_ref,
                           gated_ref, gmv_ref, xb_s, z_s, *, tm, tn, inner, dec_seq):
    _gmlp_project(x_ref, w_ref, b_ref, xb_s, z_s)
    n_half = inner // tn

    @pl.when(pl.program_id(1) == pl.num_programs(1) - 1)
    def _():
        vn = _gmlp_v_layer_norm(z_s, lng_ref, lnb_ref, slice(0, tm), n_half, tn, inner)
        t_idx = lax.broadcasted_iota(jnp.int32, (dec_seq, tn), 0)
        for c in range(n_half):
            cols = slice(c * tn, (c + 1) * tn)
            gmv_ref[:, cols] = vn[c]
            for b in range(tm // dec_seq):
                rows = slice(b * dec_seq, (b + 1) * dec_seq)
                vb = vn[c][rows, :]
                mixed = bexp_ref[:, cols]
                for s in range(dec_seq):
                    w_ts = jnp.where(t_idx >= s, wexp_ref[s, :, cols], 0.0)
                    mixed = mixed + w_ts * vb[s:s + 1, :]
                gated_ref[rows, cols] = (z_s[c, rows, :] * mixed).astype(BF16)


def _gmlp_in_sample(x, w_in, b_in, ln_g, ln_b, wexp, bexp, layer, dec_seq, *, tn):
    m_rows, d_model = x.shape
    inner = w_in.shape[2] // 2
    assert dec_seq % 8 == 0 and dec_seq <= CHUNK and inner % tn == 0
    nj = 2 * inner // tn
    kern = functools.partial(_gmlp_in_sample_kernel, tm=m_rows, tn=tn, inner=inner, dec_seq=dec_seq)
    return pl.pallas_call(
        kern,
        grid=(1, nj),
        in_specs=[
            pl.BlockSpec((m_rows, d_model), lambda i, j: (0, 0)),
            pl.BlockSpec((None, d_model, tn), lambda i, j: (layer, 0, j)),
            pl.BlockSpec((None, 1, tn), lambda i, j: (layer, 0, j)),
            pl.BlockSpec((None, 1, inner), lambda i, j: (layer, 0, 0)),
            pl.BlockSpec((None, 1, inner), lambda i, j: (layer, 0, 0)),
            pl.BlockSpec((dec_seq, dec_seq, inner), lambda i, j: (0, 0, 0)),
            pl.BlockSpec((dec_seq, inner), lambda i, j: (0, 0)),
        ],
        out_specs=[
            pl.BlockSpec((m_rows, inner), lambda i, j: (0, 0)),
            pl.BlockSpec((m_rows, inner), lambda i, j: (0, 0)),
        ],
        out_shape=[
            jax.ShapeDtypeStruct((m_rows, inner), BF16),
            jax.ShapeDtypeStruct((m_rows, inner), F32),
        ],
        scratch_shapes=[
            pltpu.VMEM((m_rows, d_model), BF16),
            pltpu.VMEM((nj, m_rows, tn), F32),
        ],
        compiler_params=_params("arbitrary", "arbitrary"),
    )(x, w_in, b_in.reshape(b_in.shape[0], 1, -1), ln_g.reshape(ln_g.shape[0], 1, -1),
      ln_b.reshape(ln_b.shape[0], 1, -1), wexp, bexp)


def _proj_ln_kernel(*refs, alpha, tm, has_bias):
    if has_bias:
        a_ref, w_ref, bias_ref, x_ref, g_ref, b_ref, o_ref, wb_s = refs
    else:
        a_ref, w_ref, x_ref, g_ref, b_ref, o_ref, wb_s = refs
        bias_ref = None

    @pl.when(pl.program_id(0) == 0)
    def _():
        _cast_rows_to_bf16(wb_s, w_ref)

    o_ref[...] = jnp.dot(a_ref[...].astype(BF16), wb_s[...], preferred_element_type=F32)
    _residual_ln_inplace(o_ref, x_ref, bias_ref, g_ref, b_ref, alpha, tm)


def _proj_ln(a, w, layer, bias, x, ln_g, ln_b, ln_idx, *, alpha, tm):
    m_rows, k_dim = a.shape
    d_model = x.shape[1]
    assert m_rows % tm == 0
    vec = lambda idx: pl.BlockSpec((None, 1, d_model), lambda i: (idx, 0, 0))
    in_specs = [pl.BlockSpec((tm, k_dim), lambda i: (i, 0)),
                pl.BlockSpec((None, k_dim, d_model), lambda i: (layer, 0, 0), pipeline_mode=pl.Buffered(1))]
    args = [a, w]
    if bias is not None:
        in_specs.append(vec(layer))
        args.append(bias.reshape(bias.shape[0], 1, -1))
    in_specs += [pl.BlockSpec((tm, d_model), lambda i: (i, 0)), vec(ln_idx), vec(ln_idx)]
    args += [x, ln_g.reshape(ln_g.shape[0], 1, -1), ln_b.reshape(ln_b.shape[0], 1, -1)]
    kern = functools.partial(_proj_ln_kernel, alpha=alpha, tm=tm, has_bias=bias is not None)
    return pl.pallas_call(
        kern,
        grid=(m_rows // tm,),
        in_specs=in_specs,
        out_specs=pl.BlockSpec((tm, d_model), lambda i: (i, 0)),
        out_shape=jax.ShapeDtypeStruct((m_rows, d_model), F32),
        scratch_shapes=[pltpu.VMEM((k_dim, d_model), BF16)],
        compiler_params=_params("arbitrary"),
    )(*args)


def _ffn_kernel(x_ref, wg_ref, wu_ref, wo_ref, g_ref, b_ref, o_ref, xb_s, *, alpha, tm):
    f = pl.program_id(1)

    @pl.when(f == 0)
    def _():
        xb_s[...] = x_ref[...].astype(BF16)

    xb = xb_s[...]
    gate = jnp.dot(xb, wg_ref[...].astype(BF16), preferred_element_type=F32)
    up = jnp.dot(xb, wu_ref[...].astype(BF16), preferred_element_type=F32)
    h = (jax.nn.silu(gate) * up).astype(BF16)
    part = jnp.dot(h, wo_ref[...].astype(BF16), preferred_element_type=F32)

    @pl.when(f == 0)
    def _():
        o_ref[...] = part

    @pl.when(f > 0)
    def _():
        o_ref[...] += part

    @pl.when(f == pl.num_programs(1) - 1)
    def _():
        _residual_ln_inplace(o_ref, x_ref, None, g_ref, b_ref, alpha, tm)


def _ffn(x, w_in, w_out, layer, ln_g, ln_b, *, alpha, tm, tf):
    m_rows, d_model = x.shape
    d_ff = w_out.shape[1]
    assert m_rows % tm == 0 and d_ff % tf == 0
    nf = d_ff // tf
    vec = pl.BlockSpec((None, 1, d_model), lambda i, f: (layer, 0, 0))
    kern = functools.partial(_ffn_kernel, alpha=alpha, tm=tm)
    return pl.pallas_call(
        kern,
        grid=(m_rows // tm, nf),
        in_specs=[
            pl.BlockSpec((tm, d_model), lambda i, f: (i, 0)),
            pl.BlockSpec((None, d_model, tf), lambda i, f: (layer, 0, f)),
            pl.BlockSpec((None, d_model, tf), lambda i, f: (layer, 0, nf + f)),
            pl.BlockSpec((None, tf, d_model), lambda i, f: (layer, f, 0)),
            vec, vec,
        ],
        out_specs=pl.BlockSpec((tm, d_model), lambda i, f: (i, 0)),
        out_shape=jax.ShapeDtypeStruct((m_rows, d_model), F32),
        scratch_shapes=[pltpu.VMEM((tm, d_model), BF16)],
        compiler_params=_params("arbitrary", "arbitrary"),
    )(x, w_in, w_in, w_out, ln_g.reshape(ln_g.shape[0], 1, -1), ln_b.reshape(ln_b.shape[0], 1, -1))


def _rope_tables(pos):
    half = ROT_DIM // 2
    inv = jnp.power(ROPE_THETA, -jnp.arange(half, dtype=F32) * 2.0 / ROT_DIM)
    ang = pos.astype(F32)[:, None] * inv[None, :]
    cos, sin = jnp.cos(ang), jnp.sin(ang)
    n = pos.shape[0]
    rest = HEAD_DIM - ROT_DIM
    c = jnp.concatenate([cos, cos, jnp.ones((n, rest), F32)], axis=1)
    s_next = jnp.concatenate([-sin, jnp.zeros((n, half + rest), F32)], axis=1)
    s_prev = jnp.concatenate([jnp.zeros((n, half), F32), sin, jnp.zeros((n, rest), F32)], axis=1)
    return tuple(jnp.tile(t, (1, HEAD_W // HEAD_DIM)) for t in (c, s_next, s_prev))


def _proj_rope_kernel(*refs, rope, scale, n_out, vt_tk):
    n_in = 5 if rope else 2
    x_ref, w_ref = refs[:2]
    out_refs = refs[n_in:n_in + n_out]
    vt_ref = refs[n_in + n_out] if vt_tk else None
    wb_s = refs[-1]

    @pl.when(pl.program_id(0) == 0)
    def _():
        _cast_rows_to_bf16(wb_s, w_ref)

    y = jnp.dot(x_ref[...].astype(BF16), wb_s[...], preferred_element_type=F32)
    for h in range(y.shape[1] // HEAD_W):
        cols = slice(h * HEAD_W, (h + 1) * HEAD_W)
        yh = y[:, cols]
        if rope:
            c_ref, sn_ref, sp_ref = refs[2:5]
            half = ROT_DIM // 2
            yh = (yh * c_ref[...] + pltpu.roll(yh, HEAD_W - half, 1) * sn_ref[...]
                  + pltpu.roll(yh, half, 1) * sp_ref[...])
        if scale != 1.0:
            yh = yh * scale
        for o_ref in out_refs:
            o_ref[:, cols] = yh.astype(o_ref.dtype)
        if vt_ref is not None:
            for kb in range(y.shape[0] // vt_tk):
                vt_ref[h, kb] = yh[kb * vt_tk:(kb + 1) * vt_tk, :].T.astype(BF16)


def _proj_rope(x, w, layer, col0, n_cols, tables, pos_blocks, out_dtypes, *, scale, tm, vt=None):
    m_rows, d_model = x.shape
    assert m_rows % tm == 0 and col0 % n_cols == 0 and n_cols % HEAD_W == 0
    rope = tables is not None
    in_specs = [pl.BlockSpec((tm, d_model), lambda i: (i, 0)),
                pl.BlockSpec((None, d_model, n_cols), lambda i: (layer, 0, col0 // n_cols),
                             pipeline_mode=pl.Buffered(1))]
    args = [x, w]
    if rope:
        in_specs += [pl.BlockSpec((tm, HEAD_W), lambda i: (i % pos_blocks, 0))] * 3
        args += list(tables)
    out_specs = [pl.BlockSpec((tm, n_cols), lambda i: (i, 0)) for _ in out_dtypes]
    out_shape = [jax.ShapeDtypeStruct((m_rows, n_cols), dt) for dt in out_dtypes]
    vt_tk = 0
    if vt is not None:
        seq, vt_tk = vt
        assert seq % tm == 0 and tm % vt_tk == 0
        tiles_per_seq, n_heads = seq // tm, n_cols // HEAD_W
        out_specs.append(pl.BlockSpec((n_heads, tm // vt_tk, HEAD_W, vt_tk),
                                      lambda i: (i // tiles_per_seq, i % tiles_per_seq, 0, 0)))
        out_shape.append(jax.ShapeDtypeStruct(((m_rows // seq) * n_heads, seq // vt_tk, HEAD_W, vt_tk), BF16))
    kern = functools.partial(_proj_rope_kernel, rope=rope, scale=scale, n_out=len(out_dtypes), vt_tk=vt_tk)
    return pl.pallas_call(
        kern,
        grid=(m_rows // tm,),
        in_specs=in_specs,
        out_specs=out_specs,
        out_shape=out_shape,
        scratch_shapes=[pltpu.VMEM((d_model, n_cols), BF16)],
        compiler_params=_params("arbitrary"),
    )(*args)


def _diff_lambda(lam_ref, lam_init):
    lv = lam_ref[...]
    e1 = jnp.exp(jnp.sum(lv[0:1, :] * lv[1:2, :], axis=-1, keepdims=True))
    e2 = jnp.exp(jnp.sum(lv[2:3, :] * lv[3:4, :], axis=-1, keepdims=True))
    return e1 - e2 + lam_init


def _map_masks(rows):
    lane = lax.broadcasted_iota(jnp.int32, (rows, HEAD_W), 1)
    return lane < HEAD_DIM, lane >= HEAD_DIM


def _sub_ln(o, g_col, lam_init):
    return o * lax.rsqrt(jnp.mean(o * o, axis=0, keepdims=True) + LN_EPS) * g_col * (1.0 - lam_init)


def _attn_prompt_kernel(q_ref, k_ref, vt_ref, lam_ref, g_ref, o_ref, q2_s, m_s, l_s, acc_s, c_s, mx_s, s_s, p_s,
                        *, tq, lam_init):
    qi = pl.program_id(2)
    q = q_ref[...]
    m0, m1 = _map_masks(tq)
    zero = jnp.zeros_like(q)
    q2_s[0:tq, :] = jnp.where(m0, q, zero)
    q2_s[tq:2 * tq, :] = jnp.where(m1, q, zero)
    m_s[...] = jnp.full_like(m_s, -jnp.inf)
    l_s[...] = jnp.zeros_like(l_s)
    acc_s[...] = jnp.zeros_like(acc_s)
    gw = min(MXU_COLS, tq)

    def block(ki, diagonal):
        kb = k_ref[pl.ds(pl.multiple_of(ki * tq, tq), tq), :]
        vtb = vt_ref[ki]
        n_g = 2 * tq // gw
        n_keys = [min(tq, (g * gw) % tq + gw) if diagonal else tq for g in range(n_g)]
        for g in range(n_g):
            cols, nk = slice(g * gw, (g + 1) * gw), n_keys[g]
            s = lax.dot_general(kb[:nk], q2_s[cols, :], (((1,), (1,)), ((), ())), preferred_element_type=F32)
            if diagonal:
                key = lax.broadcasted_iota(jnp.int32, (nk, gw), 0)
                qry = lax.broadcasted_iota(jnp.int32, (nk, gw), 1) + (g * gw) % tq
                s = jnp.where(key <= qry, s, -jnp.inf)
            s_s[g, 0:nk, :] = s
            mx_s[:, cols] = jnp.max(s, axis=0, keepdims=True)
        for g in range(n_g):
            cols, nk = slice(g * gw, (g + 1) * gw), n_keys[g]
            m_old = m_s[:, cols]
            m_new = jnp.maximum(m_old, mx_s[:, cols])
            c_s[:, cols] = jnp.exp(m_old - m_new)
            p_sum = jnp.zeros((1, gw), F32)
            for r in range(0, nk, SOFTMAX_ROWS):
                p = jnp.exp(s_s[g, r:r + SOFTMAX_ROWS, :] - m_new)
                p_sum = p_sum + jnp.sum(p, axis=0, keepdims=True)
                p_s[g, r:r + SOFTMAX_ROWS, :] = p.astype(BF16)
            l_s[:, cols] = c_s[:, cols] * l_s[:, cols] + p_sum
            m_s[:, cols] = m_new
        for g in range(n_g):
            cols, nk = slice(g * gw, (g + 1) * gw), n_keys[g]
            acc_s[:, cols] = c_s[:, cols] * acc_s[:, cols] + jnp.dot(vtb[:, :nk], p_s[g, 0:nk, :],
                                                                     preferred_element_type=F32)

    def body(ki, carry):
        block(ki, False)
        return carry

    lax.fori_loop(0, qi, body, 0)
    block(qi, True)

    o = acc_s[...] / l_s[...]
    o = o[:, :tq] - _diff_lambda(lam_ref, lam_init) * o[:, tq:]
    o_ref[...] = _sub_ln(o, g_ref[...], lam_init).T.astype(o_ref.dtype)


def _attn_prompt(qb, kb, vt, lamv, subln_col, n_batch, seq, *, lam_init, tq):
    m_rows, d_model = qb.shape
    n_heads = d_model // HEAD_W
    assert seq % tq == 0 and vt.shape[1:] == (seq // tq, HEAD_W, tq)
    nq = seq // tq
    gw = min(MXU_COLS, tq)
    kern = functools.partial(_attn_prompt_kernel, tq=tq, lam_init=lam_init)
    return pl.pallas_call(
        kern,
        grid=(n_batch, n_heads, nq),
        in_specs=[
            pl.BlockSpec((tq, HEAD_W), lambda b, h, qi: (b * nq + qi, h)),
            pl.BlockSpec((seq, HEAD_W), lambda b, h, qi: (b, h)),
            pl.BlockSpec((None, nq, HEAD_W, tq), lambda b, h, qi: (b * n_heads + h, 0, 0, 0)),
            pl.BlockSpec((4, HEAD_DIM), lambda b, h, qi: (0, 0)),
            pl.BlockSpec((HEAD_W, 1), lambda b, h, qi: (0, 0)),
        ],
        out_specs=pl.BlockSpec((tq, HEAD_W), lambda b, h, qi: (b * nq + qi, h)),
        out_shape=jax.ShapeDtypeStruct((m_rows, d_model), BF16),
        scratch_shapes=[
            pltpu.VMEM((2 * tq, HEAD_W), BF16),
            pltpu.VMEM((1, 2 * tq), F32),
            pltpu.VMEM((1, 2 * tq), F32),
            pltpu.VMEM((HEAD_W, 2 * tq), F32),
            pltpu.VMEM((1, 2 * tq), F32),
            pltpu.VMEM((1, 2 * tq), F32),
            pltpu.VMEM((2 * tq // gw, tq, gw), F32),
            pltpu.VMEM((2 * tq // gw, tq, gw), BF16),
        ],
        compiler_params=_params("arbitrary", "arbitrary", "arbitrary"),
    )(qb, kb, vt, lamv, subln_col)


def _attn_sample_kernel(pt_ref, q_ref, *refs, n_heads, n_slots, dec_seq, lam_init):
    del pt_ref
    k_pages = refs[:n_slots]
    v_pages = refs[n_slots:2 * n_slots]
    kn_ref, vn_ref, lam_ref, g_ref, o_ref, qm_s, m_s, l_s, acc_s = refs[2 * n_slots:]
    j = pl.program_id(1)
    hp = min(n_heads, SUBLANES)
    n_groups = n_heads // hp
    cw = 2 * dec_seq
    width = hp * cw

    @pl.when(j == 0)
    def _():
        m0, m1 = _map_masks(dec_seq)
        for h in range(n_heads):
            grp, hl = divmod(h, hp)
            qh = q_ref[:, h * HEAD_W:(h + 1) * HEAD_W]
            qm_s[grp, hl * cw:(hl + 1) * cw, :] = jnp.concatenate(
                [jnp.where(m0, qh, 0.0), jnp.where(m1, qh, 0.0)], axis=0).astype(BF16)
        m_s[...] = jnp.full_like(m_s, -jnp.inf)
        l_s[...] = jnp.zeros_like(l_s)
        acc_s[...] = jnp.zeros_like(acc_s)

    own = (lax.broadcasted_iota(jnp.int32, (hp, width), 1) // cw
           == lax.broadcasted_iota(jnp.int32, (hp, width), 0))

    def update(slot, k3_ref, v3_ref, causal):
        n_keys = k3_ref.shape[0]
        rows = n_keys * hp
        for grp in range(n_groups):
            heads = slice(grp * hp, (grp + 1) * hp)
            kr = k3_ref[:, heads, :].reshape(rows, HEAD_W).astype(BF16)
            u = lax.dot_general(kr, qm_s[grp], (((1,), (1,)), ((), ())), preferred_element_type=F32)
            u = u.reshape(n_keys, hp, width)
            if causal:
                key = lax.broadcasted_iota(jnp.int32, (n_keys, hp, width), 0)
                qry = lax.broadcasted_iota(jnp.int32, (n_keys, hp, width), 2) % dec_seq
                u = jnp.where(key <= qry, u, -jnp.inf)
            m_old = m_s[slot, grp]
            m_new = jnp.maximum(m_old, jnp.max(u, axis=0))
            p = jnp.where(own, jnp.exp(u - m_new), 0.0)
            corr = jnp.exp(m_old - m_new)
            l_s[slot, grp] = corr * l_s[slot, grp] + jnp.sum(p, axis=0)
            m_s[slot, grp] = m_new
            vr = v3_ref[:, heads, :].reshape(rows, HEAD_W).astype(BF16)
            pv = lax.dot_general(vr, p.reshape(rows, width).astype(BF16), (((0,), (0,)), ((), ())),
                                 preferred_element_type=F32)
            corr_col = jnp.sum(jnp.where(own, corr, 0.0), axis=0, keepdims=True)
            acc_s[slot, grp] = acc_s[slot, grp] * corr_col + pv

    for slot in range(n_slots):
        update(slot, k_pages[slot], v_pages[slot], False)

    @pl.when(j == pl.num_programs(1) - 1)
    def _():
        update(0, kn_ref, vn_ref, True)
        lam = _diff_lambda(lam_ref, lam_init)
        for grp in range(n_groups):
            m_all = m_s[0, grp]
            for slot in range(1, n_slots):
                m_all = jnp.maximum(m_all, m_s[slot, grp])
            l_all = jnp.zeros((hp, width), F32)
            acc = jnp.zeros((HEAD_W, width), F32)
            for slot in range(n_slots):
                w = jnp.exp(m_s[slot, grp] - m_all)
                l_all = l_all + w * l_s[slot, grp]
                acc = acc + acc_s[slot, grp] * jnp.sum(jnp.where(own, w, 0.0), axis=0, keepdims=True)
            o = acc / jnp.sum(jnp.where(own, l_all, 0.0), axis=0, keepdims=True)
            o = o - lam * pltpu.roll(o, width - dec_seq, 1)
            ot = _sub_ln(o, g_ref[...], lam_init).T
            for hl in range(hp):
                h = grp * hp + hl
                o_ref[:, h * HEAD_W:(h + 1) * HEAD_W] = ot[hl * cw:hl * cw + dec_seq, :]


def _attn_sample(q, cache_k, cache_v, page_table, kn, vn, lamv, subln_col, layer, dec_seq, *,
                 lam_init, n_slots):
    m_rows, d_model = q.shape
    n_heads = d_model // HEAD_W
    n_dec, n_pages = page_table.shape
    page = cache_k.shape[2]
    hp = min(n_heads, SUBLANES)
    assert n_pages % n_slots == 0 and dec_seq % SUBLANES == 0 and n_heads % hp == 0
    width = hp * 2 * dec_seq
    n_groups = n_heads // hp

    def page_spec(slot):
        return pl.BlockSpec((None, None, page, n_heads, HEAD_W),
                            lambda b, j, pt: (layer, pt[b, j * n_slots + slot], 0, 0, 0))

    row_spec = pl.BlockSpec((dec_seq, d_model), lambda b, j, pt: (b, 0))
    new_spec = pl.BlockSpec((None, dec_seq, n_heads, HEAD_W), lambda b, j, pt: (b, 0, 0, 0))
    kern = functools.partial(_attn_sample_kernel, n_heads=n_heads, n_slots=n_slots, dec_seq=dec_seq,
                             lam_init=lam_init)
    grid_spec = pltpu.PrefetchScalarGridSpec(
        num_scalar_prefetch=1,
        grid=(n_dec, n_pages // n_slots),
        in_specs=([row_spec] + [page_spec(s) for s in range(n_slots)] * 2 + [new_spec, new_spec]
                  + [pl.BlockSpec((4, HEAD_DIM), lambda b, j, pt: (0, 0)),
                     pl.BlockSpec((HEAD_W, 1), lambda b, j, pt: (0, 0))]),
        out_specs=row_spec,
        scratch_shapes=[
            pltpu.VMEM((n_groups, width, HEAD_W), BF16),
            pltpu.VMEM((n_slots, n_groups, hp, width), F32),
            pltpu.VMEM((n_slots, n_groups, hp, width), F32),
            pltpu.VMEM((n_slots, n_groups, HEAD_W, width), F32),
        ],
    )
    new_rows = lambda a: a.reshape(n_dec, dec_seq, n_heads, HEAD_W)
    return pl.pallas_call(
        kern,
        grid_spec=grid_spec,
        out_shape=jax.ShapeDtypeStruct((m_rows, d_model), F32),
        compiler_params=_params("arbitrary", "arbitrary"),
    )(page_table, q, *([cache_k] * n_slots), *([cache_v] * n_slots), new_rows(kn), new_rows(vn), lamv, subln_col)


def _tile(n, target):
    t = min(n, target)
    assert n % t == 0
    return t


def kernel(x_prompt, x_sample, cache_k, cache_v, page_table, gm_w_in, gm_b_in, gm_ln_g, gm_ln_b, gm_w_s, gm_b_s,
           gm_w_out, gm_b_out, at_w_qkv, at_lambda_q1, at_lambda_k1, at_lambda_q2, at_lambda_k2, at_subln_g,
           at_w_out, ln_mix_g, ln_mix_b, ln_ffn_g, ln_ffn_b, ffn_w_in, ffn_w_out):
    n_batch, seq, d_model = x_prompt.shape
    n_dec, dec_seq, _ = x_sample.shape
    depth = ln_mix_g.shape[0]
    n_heads = d_model // HEAD_W
    n_pages = page_table.shape[1]
    past_len = n_pages * cache_k.shape[2]
    alpha = (2 * depth) ** 0.25
    inner = gm_w_in.shape[2] // 2
    d_ff = ffn_w_out.shape[1]

    xp = x_prompt.reshape(n_batch * seq, d_model)
    xs = x_sample.reshape(n_dec * dec_seq, d_model)
    mp, ms = xp.shape[0], xs.shape[0]

    tm_p = _tile(seq, 1024)
    tm_r = _tile(seq, 512)
    tm_g = _tile(seq, 512)
    tn = _tile(inner, 512)
    tf = 256 if d_ff % 256 == 0 else d_ff

    tables_p = _rope_tables(jnp.arange(seq, dtype=jnp.int32))
    tables_s = _rope_tables(jnp.tile(past_len + jnp.arange(dec_seq, dtype=jnp.int32), n_dec))

    gm_v_p, gm_v_s, k_p, v_p, k_s, v_s = [], [], [], [], [], []
    for i in range(depth):
        j = i // 2
        if i % 2 == 0:
            bexp = jnp.repeat(gm_b_s[j].T, GROUP_W, axis=1)
            wexp = jnp.repeat(jnp.transpose(gm_w_s[j][:, :dec_seq, :dec_seq], (2, 1, 0)), GROUP_W, axis=2)
            gated_p, gv_p = _gmlp_in_prompt(xp, gm_w_in, gm_b_in, gm_ln_g, gm_ln_b, gm_w_s, bexp, j, seq,
                                            tm=tm_g, tn=tn)
            gated_s, gv_s = _gmlp_in_sample(xs, gm_w_in, gm_b_in, gm_ln_g, gm_ln_b, wexp, bexp[:dec_seq], j,
                                            dec_seq, tn=tn)
            gm_v_p.append(gv_p)
            gm_v_s.append(gv_s.reshape(n_dec, dec_seq, inner))
            mix_p = dict(a=gated_p, w=gm_w_out, bias=gm_b_out)
            mix_s = dict(a=gated_s, w=gm_w_out, bias=gm_b_out)
        else:
            lam_init = 0.8 - 0.6 * math.exp(-0.3 * i)
            lamv = jnp.stack([at_lambda_q1[j], at_lambda_k1[j], at_lambda_q2[j], at_lambda_k2[j]])
            subln = at_subln_g[j].reshape(HEAD_W, 1)
            scale = HEAD_DIM ** -0.5
            tq = _tile(seq, 512)
            proj = functools.partial(_proj_rope, w=at_w_qkv, layer=j, n_cols=d_model)
            (qb,) = proj(xp, col0=0, tables=tables_p, pos_blocks=seq // tm_r, out_dtypes=(BF16,),
                         scale=scale, tm=tm_r)
            kf, kb = proj(xp, col0=d_model, tables=tables_p, pos_blocks=seq // tm_r, out_dtypes=(F32, BF16),
                          scale=1.0, tm=tm_r)
            vf, vt = proj(xp, col0=2 * d_model, tables=None, pos_blocks=1, out_dtypes=(F32,),
                          scale=1.0, tm=tm_r, vt=(seq, tq))
            a_p = _attn_prompt(qb, kb, vt, lamv, subln, n_batch, seq, lam_init=lam_init, tq=tq)
            (qs,) = proj(xs, col0=0, tables=tables_s, pos_blocks=1, out_dtypes=(F32,), scale=scale, tm=ms)
            (kn,) = proj(xs, col0=d_model, tables=tables_s, pos_blocks=1, out_dtypes=(F32,), scale=1.0, tm=ms)
            (vn,) = proj(xs, col0=2 * d_model, tables=None, pos_blocks=1, out_dtypes=(F32,), scale=1.0, tm=ms)
            a_s = _attn_sample(qs, cache_k, cache_v, page_table, kn, vn, lamv, subln, j, dec_seq,
                               lam_init=lam_init, n_slots=math.gcd(n_pages, 8))
            k_p.append(kf.reshape(n_batch, seq, n_heads, HEAD_W))
            v_p.append(vf.reshape(n_batch, seq, n_heads, HEAD_W))
            k_s.append(kn.reshape(n_dec, dec_seq, n_heads, HEAD_W))
            v_s.append(vn.reshape(n_dec, dec_seq, n_heads, HEAD_W))
            mix_p = dict(a=a_p, w=at_w_out, bias=None)
            mix_s = dict(a=a_s, w=at_w_out, bias=None)
        xp = _proj_ln(mix_p["a"], mix_p["w"], j, mix_p["bias"], xp, ln_mix_g, ln_mix_b, i,
                      alpha=alpha, tm=tm_r)
        xs = _proj_ln(mix_s["a"], mix_s["w"], j, mix_s["bias"], xs, ln_mix_g, ln_mix_b, i,
                      alpha=alpha, tm=ms)
        xp = _ffn(xp, ffn_w_in, ffn_w_out, i, ln_ffn_g, ln_ffn_b, alpha=alpha, tm=tm_p, tf=tf)
        xs = _ffn(xs, ffn_w_in, ffn_w_out, i, ln_ffn_g, ln_ffn_b, alpha=alpha, tm=ms, tf=tf)

    return (xp.reshape(n_batch, seq, d_model), xs.reshape(n_dec, dec_seq, d_model),
            jnp.stack(gm_v_p), jnp.stack(gm_v_s), jnp.stack(k_p), jnp.stack(v_p), jnp.stack(k_s), jnp.stack(v_s))
```

```python
import functools
import math

import jax
import jax.numpy as jnp
from jax import lax
from jax.experimental import pallas as pl
from jax.experimental.pallas import tpu as pltpu

F32 = jnp.float32
BF16 = jnp.bfloat16

LN_EPS = 1e-5
CHUNK = 128
GROUP_W = 128
HEAD_DIM = 64
HEAD_W = 2 * HEAD_DIM
ROT_DIM = HEAD_DIM // 4
ROPE_THETA = 500000.0
LANES = 128
SUBLANES = 8
MXU_COLS = 256
LN_ROWS = 128
SOFTMAX_ROWS = 64
VMEM_LIMIT_BYTES = 60 * 1024 * 1024


def _params(*sem):
    return pltpu.CompilerParams(dimension_semantics=sem, vmem_limit_bytes=VMEM_LIMIT_BYTES)


def _layer_norm(y, g, b):
    mu = jnp.mean(y, axis=-1, keepdims=True)
    yc = y - mu
    var = jnp.mean(yc * yc, axis=-1, keepdims=True)
    return yc * lax.rsqrt(var + LN_EPS) * g + b


def _residual_ln_inplace(o_ref, x_ref, bias_ref, g_ref, b_ref, alpha, tm):
    block = min(tm, LN_ROWS)
    assert tm % block == 0

    def body(r, carry):
        rows = pl.ds(pl.multiple_of(r * block, block), block)
        y = alpha * x_ref[rows, :] + o_ref[rows, :]
        if bias_ref is not None:
            y = y + bias_ref[...]
        o_ref[rows, :] = _layer_norm(y, g_ref[...], b_ref[...])
        return carry

    lax.fori_loop(0, tm // block, body, 0)


def _cast_rows_to_bf16(dst_ref, src_ref):
    rows = 256
    n = src_ref.shape[0]
    assert n % rows == 0

    def body(r, carry):
        sl = pl.ds(pl.multiple_of(r * rows, rows), rows)
        dst_ref[sl, :] = src_ref[sl, :].astype(BF16)
        return carry

    lax.fori_loop(0, n // rows, body, 0)


def _gmlp_project(x_ref, w_ref, b_ref, xb_s, z_s, wb_s=None):
    j = pl.program_id(1)

    @pl.when(j == 0)
    def _():
        xb_s[...] = x_ref[...].astype(BF16)

    if wb_s is None:
        wb = w_ref[...].astype(BF16)
    else:
        @pl.when(pl.program_id(0) == 0)
        def _():
            wb_s[j] = w_ref[...].astype(BF16)

        wb = wb_s[j]
    z = jnp.dot(xb_s[...], wb, preferred_element_type=F32) + b_ref[...]
    z_s[j] = 0.5 * z * (1.0 + lax.erf(z * math.sqrt(0.5)))


def _gmlp_v_layer_norm(z_s, lng_ref, lnb_ref, rows, n_half, tn, inner):
    vs = [z_s[n_half + c, rows, :] for c in range(n_half)]
    mu = sum(jnp.sum(v, axis=-1, keepdims=True) for v in vs) / inner
    var = sum(jnp.sum(jnp.square(v - mu), axis=-1, keepdims=True) for v in vs) / inner
    rstd = lax.rsqrt(var + LN_EPS)
    return [(vs[c] - mu) * rstd * lng_ref[:, c * tn:(c + 1) * tn] + lnb_ref[:, c * tn:(c + 1) * tn]
            for c in range(n_half)]


def _gmlp_in_prompt_kernel(x_ref, w_ref, b_ref, lng_ref, lnb_ref, ws_ref, bexp_ref,
                           gated_ref, gmv_ref, xb_s, z_s, vn_s, wb_s, *, tm, tn, inner):
    _gmlp_project(x_ref, w_ref, b_ref, xb_s, z_s, wb_s)
    n_half = inner // tn
    n_groups = inner // GROUP_W
    n_rc = tm // CHUNK
    per_chunk = tn // GROUP_W

    @pl.when(pl.program_id(1) == pl.num_programs(1) - 1)
    def _():
        for r in range(n_rc):
            rows = slice(r * CHUNK, (r + 1) * CHUNK)
            vn = _gmlp_v_layer_norm(z_s, lng_ref, lnb_ref, rows, n_half, tn, inner)
            for c in range(n_half):
                if r == n_rc - 1:
                    gmv_ref[:, c * tn:(c + 1) * tn] = vn[c]
                for q in range(per_chunk):
                    g = c * per_chunk + q
                    vn_s[g, :, r * CHUNK:(r + 1) * CHUNK] = vn[c][:, q * GROUP_W:(q + 1) * GROUP_W].astype(BF16)
        t_idx = lax.broadcasted_iota(jnp.int32, (CHUNK, CHUNK), 0)
        s_idx = lax.broadcasted_iota(jnp.int32, (CHUNK, CHUNK), 1)
        causal = s_idx <= t_idx
        for g in range(n_groups):
            w_causal = jnp.where(causal, ws_ref[g], 0.0).astype(BF16)
            mixed = jnp.dot(w_causal, vn_s[g], preferred_element_type=F32)
            cols = slice(g * GROUP_W, (g + 1) * GROUP_W)
            c, q = divmod(g, per_chunk)
            for r in range(n_rc):
                rows = slice(r * CHUNK, (r + 1) * CHUNK)
                u = z_s[c, rows, q * GROUP_W:(q + 1) * GROUP_W]
                m = mixed[:, r * CHUNK:(r + 1) * CHUNK] + bexp_ref[:, cols]
                gated_ref[rows, cols] = (u * m).astype(BF16)


def _gmlp_in_prompt(x, w_in, b_in, ln_g, ln_b, w_s, bexp, layer, seq, *, tm, tn):
    m_rows, d_model = x.shape
    inner = w_in.shape[2] // 2
    n_groups = inner // GROUP_W
    n_batch = m_rows // seq
    assert seq % tm == 0 and tm % CHUNK == 0 and inner % tn == 0 and tn % GROUP_W == 0
    tiles_per_seq = seq // tm
    nj = 2 * inner // tn
    kern = functools.partial(_gmlp_in_prompt_kernel, tm=tm, tn=tn, inner=inner)
    return pl.pallas_call(
        kern,
        grid=(m_rows // tm, nj),
        in_specs=[
            pl.BlockSpec((tm, d_model), lambda i, j: (i, 0)),
            pl.BlockSpec((None, d_model, tn), lambda i, j: (layer, 0, jnp.where(i == 0, j, nj - 1))),
            pl.BlockSpec((None, 1, tn), lambda i, j: (layer, 0, j)),
            pl.BlockSpec((None, 1, inner), lambda i, j: (layer, 0, 0)),
            pl.BlockSpec((None, 1, inner), lambda i, j: (layer, 0, 0)),
            pl.BlockSpec((None, n_groups, CHUNK, CHUNK), lambda i, j: (layer, 0, 0, 0)),
            pl.BlockSpec((CHUNK, inner), lambda i, j: (0, 0)),
        ],
        out_specs=[
            pl.BlockSpec((tm, inner), lambda i, j: (i, 0)),
            pl.BlockSpec((None, CHUNK, inner), lambda i, j: (i // tiles_per_seq, 0, 0)),
        ],
        out_shape=[
            jax.ShapeDtypeStruct((m_rows, inner), BF16),
            jax.ShapeDtypeStruct((n_batch, CHUNK, inner), F32),
        ],
        scratch_shapes=[
            pltpu.VMEM((tm, d_model), BF16),
            pltpu.VMEM((nj, tm, tn), F32),
            pltpu.VMEM((n_groups, CHUNK, tm), BF16),
            pltpu.VMEM((nj, d_model, tn), BF16),
        ],
        compiler_params=_params("arbitrary", "arbitrary"),
    )(x, w_in, b_in.reshape(b_in.shape[0], 1, -1), ln_g.reshape(ln_g.shape[0], 1, -1),
      ln_b.reshape(ln_b.shape[0], 1, -1), w_s, bexp)


def _gmlp_in_sample_kernel(x_ref, w_ref, b_ref, lng_ref, lnb_ref, wexp_ref, bexp_ref,
                           gated_ref, gmv_ref, xb_s, z_s, *, tm, tn, inner, dec_seq):
    _gmlp_project(x_ref, w_ref, b_ref, xb_s, z_s)
    n_half = inner // tn

    @pl.when(pl.program_id(1) == pl.num_programs(1) - 1)
    def _():
        vn = _gmlp_v_layer_norm(z_s, lng_ref, lnb_ref, slice(0, tm), n_half, tn, inner)
        t_idx = lax.broadcasted_iota(jnp.int32, (dec_seq, tn), 0)
        for c in range(n_half):
            cols = slice(c * tn, (c + 1) * tn)
            gmv_ref[:, cols] = vn[c]
            for b in range(tm // dec_seq):
                rows = slice(b * dec_seq, (b + 1) * dec_seq)
                vb = vn[c][rows, :]
                mixed = bexp_ref[:, cols]
                for s in range(dec_seq):
                    w_ts = jnp.where(t_idx >= s, wexp_ref[s, :, cols], 0.0)
                    mixed = mixed + w_ts * vb[s:s + 1, :]
                gated_ref[rows, cols] = (z_s[c, rows, :] * mixed).astype(BF16)


def _gmlp_in_sample(x, w_in, b_in, ln_g, ln_b, wexp, bexp, layer, dec_seq, *, tn):
    m_rows, d_model = x.shape
    inner = w_in.shape[2] // 2
    assert dec_seq % 8 == 0 and dec_seq <= CHUNK and inner % tn == 0
    nj = 2 * inner // tn
    kern = functools.partial(_gmlp_in_sample_kernel, tm=m_rows, tn=tn, inner=inner, dec_seq=dec_seq)
    return pl.pallas_call(
        kern,
        grid=(1, nj),
        in_specs=[
            pl.BlockSpec((m_rows, d_model), lambda i, j: (0, 0)),
            pl.BlockSpec((None, d_model, tn), lambda i, j: (layer, 0, j)),
            pl.BlockSpec((None, 1, tn), lambda i, j: (layer, 0, j)),
            pl.BlockSpec((None, 1, inner), lambda i, j: (layer, 0, 0)),
            pl.BlockSpec((None, 1, inner), lambda i, j: (layer, 0, 0)),
            pl.BlockSpec((dec_seq, dec_seq, inner), lambda i, j: (0, 0, 0)),
            pl.BlockSpec((dec_seq, inner), lambda i, j: (0, 0)),
        ],
        out_specs=[
            pl.BlockSpec((m_rows, inner), lambda i, j: (0, 0)),
            pl.BlockSpec((m_rows, inner), lambda i, j: (0, 0)),
        ],
        out_shape=[
            jax.ShapeDtypeStruct((m_rows, inner), BF16),
            jax.ShapeDtypeStruct((m_rows, inner), F32),
        ],
        scratch_shapes=[
            pltpu.VMEM((m_rows, d_model), BF16),
            pltpu.VMEM((nj, m_rows, tn), F32),
        ],
        compiler_params=_params("arbitrary", "arbitrary"),
    )(x, w_in, b_in.reshape(b_in.shape[0], 1, -1), ln_g.reshape(ln_g.shape[0], 1, -1),
      ln_b.reshape(ln_b.shape[0], 1, -1), wexp, bexp)


def _proj_ln_kernel(*refs, alpha, has_bias):
    if has_bias:
        a_ref, as_ref, w_ref, bias_ref, x_ref, xs_ref, g_ref, b_ref, o_ref, os_ref, wb_s = refs
    else:
        a_ref, as_ref, w_ref, x_ref, xs_ref, g_ref, b_ref, o_ref, os_ref, wb_s = refs
        bias_ref = None

    def apply(a, x_in, o):
        o[...] = jnp.dot(a[...].astype(BF16), wb_s[...], preferred_element_type=F32)
        _residual_ln_inplace(o, x_in, bias_ref, g_ref, b_ref, alpha, o.shape[0])

    @pl.when(pl.program_id(0) == 0)
    def _():
        _cast_rows_to_bf16(wb_s, w_ref)
        apply(as_ref, xs_ref, os_ref)

    apply(a_ref, x_ref, o_ref)


def _proj_ln(a, a_s, w, layer, bias, x, xs, ln_g, ln_b, ln_idx, *, alpha, tm):
    m_rows, k_dim = a.shape
    ms = a_s.shape[0]
    d_model = x.shape[1]
    assert m_rows % tm == 0
    vec = lambda idx: pl.BlockSpec((None, 1, d_model), lambda i: (idx, 0, 0))
    in_specs = [pl.BlockSpec((tm, k_dim), lambda i: (i, 0)),
                pl.BlockSpec((ms, k_dim), lambda i: (0, 0)),
                pl.BlockSpec((None, k_dim, d_model), lambda i: (layer, 0, 0), pipeline_mode=pl.Buffered(1))]
    args = [a, a_s, w]
    if bias is not None:
        in_specs.append(vec(layer))
        args.append(bias.reshape(bias.shape[0], 1, -1))
    in_specs += [pl.BlockSpec((tm, d_model), lambda i: (i, 0)), pl.BlockSpec((ms, d_model), lambda i: (0, 0)),
                 vec(ln_idx), vec(ln_idx)]
    args += [x, xs, ln_g.reshape(ln_g.shape[0], 1, -1), ln_b.reshape(ln_b.shape[0], 1, -1)]
    kern = functools.partial(_proj_ln_kernel, alpha=alpha, has_bias=bias is not None)
    return pl.pallas_call(
        kern,
        grid=(m_rows // tm,),
        in_specs=in_specs,
        out_specs=[pl.BlockSpec((tm, d_model), lambda i: (i, 0)), pl.BlockSpec((ms, d_model), lambda i: (0, 0))],
        out_shape=[jax.ShapeDtypeStruct((m_rows, d_model), F32), jax.ShapeDtypeStruct((ms, d_model), F32)],
        scratch_shapes=[pltpu.VMEM((k_dim, d_model), BF16)],
        compiler_params=_params("arbitrary"),
    )(*args)


def _ffn_kernel(x_ref, xs_ref, wg_ref, wu_ref, wo_ref, g_ref, b_ref, o_ref, os_ref, xb_s, xsb_s, *, alpha):
    i, f = pl.program_id(0), pl.program_id(1)
    last = pl.num_programs(1) - 1

    def start(x_in, xb, o):
        @pl.when(f == 0)
        def _():
            xb[...] = x_in[...].astype(BF16)
            o[...] = jnp.zeros_like(o)

    def accumulate(xb, o, wg, wu, wo):
        gate = jnp.dot(xb[...], wg, preferred_element_type=F32)
        up = jnp.dot(xb[...], wu, preferred_element_type=F32)
        h = (jax.nn.silu(gate) * up).astype(BF16)
        o[...] += jnp.dot(h, wo, preferred_element_type=F32)

    def finish(x_in, o):
        @pl.when(f == last)
        def _():
            _residual_ln_inplace(o, x_in, None, g_ref, b_ref, alpha, o.shape[0])

    start(x_ref, xb_s, o_ref)
    wg = wg_ref[...].astype(BF16)
    wu = wu_ref[...].astype(BF16)
    wo = wo_ref[...].astype(BF16)
    accumulate(xb_s, o_ref, wg, wu, wo)
    finish(x_ref, o_ref)

    @pl.when(i == 0)
    def _():
        start(xs_ref, xsb_s, os_ref)
        accumulate(xsb_s, os_ref, wg, wu, wo)
        finish(xs_ref, os_ref)


def _ffn(x, xs, w_in, w_out, layer, ln_g, ln_b, *, alpha, tm, tf):
    m_rows, d_model = x.shape
    ms = xs.shape[0]
    d_ff = w_out.shape[1]
    assert m_rows % tm == 0 and d_ff % tf == 0
    nf = d_ff // tf
    vec = pl.BlockSpec((None, 1, d_model), lambda i, f: (layer, 0, 0))
    kern = functools.partial(_ffn_kernel, alpha=alpha)
    return pl.pallas_call(
        kern,
        grid=(m_rows // tm, nf),
        in_specs=[
            pl.BlockSpec((tm, d_model), lambda i, f: (i, 0)),
            pl.BlockSpec((ms, d_model), lambda i, f: (0, 0)),
            pl.BlockSpec((None, d_model, tf), lambda i, f: (layer, 0, f)),
            pl.BlockSpec((None, d_model, tf), lambda i, f: (layer, 0, nf + f)),
            pl.BlockSpec((None, tf, d_model), lambda i, f: (layer, f, 0)),
            vec, vec,
        ],
        out_specs=[pl.BlockSpec((tm, d_model), lambda i, f: (i, 0)),
                   pl.BlockSpec((ms, d_model), lambda i, f: (0, 0))],
        out_shape=[jax.ShapeDtypeStruct((m_rows, d_model), F32), jax.ShapeDtypeStruct((ms, d_model), F32)],
        scratch_shapes=[pltpu.VMEM((tm, d_model), BF16), pltpu.VMEM((ms, d_model), BF16)],
        compiler_params=_params("arbitrary", "arbitrary"),
    )(x, xs, w_in, w_in, w_out, ln_g.reshape(ln_g.shape[0], 1, -1), ln_b.reshape(ln_b.shape[0], 1, -1))


def _rope_tables(pos):
    half = ROT_DIM // 2
    inv = jnp.power(ROPE_THETA, -jnp.arange(half, dtype=F32) * 2.0 / ROT_DIM)
    ang = pos.astype(F32)[:, None] * inv[None, :]
    cos, sin = jnp.cos(ang), jnp.sin(ang)
    n = pos.shape[0]
    rest = HEAD_DIM - ROT_DIM
    c = jnp.concatenate([cos, cos, jnp.ones((n, rest), F32)], axis=1)
    s_next = jnp.concatenate([-sin, jnp.zeros((n, half + rest), F32)], axis=1)
    s_prev = jnp.concatenate([jnp.zeros((n, half), F32), sin, jnp.zeros((n, rest), F32)], axis=1)
    return tuple(jnp.tile(t, (1, HEAD_W // HEAD_DIM)) for t in (c, s_next, s_prev))


def _proj_rope_kernel(*refs, rope, scale, n_out, vt_tk):
    n_tab = 3 if rope else 0
    x_ref, xs_ref, w_ref = refs[:3]
    tabs, tabs_s = refs[3:3 + n_tab], refs[3 + n_tab:3 + 2 * n_tab]
    k = 3 + 2 * n_tab
    out_refs, os_ref = refs[k:k + n_out], refs[k + n_out]
    vt_ref = refs[k + n_out + 1] if vt_tk else None
    wb_s = refs[-1]

    def emit(x_in, tables, outs, vt):
        y = jnp.dot(x_in[...].astype(BF16), wb_s[...], preferred_element_type=F32)
        for h in range(y.shape[1] // HEAD_W):
            cols = slice(h * HEAD_W, (h + 1) * HEAD_W)
            yh = y[:, cols]
            if rope:
                c_ref, sn_ref, sp_ref = tables
                half = ROT_DIM // 2
                yh = (yh * c_ref[...] + pltpu.roll(yh, HEAD_W - half, 1) * sn_ref[...]
                      + pltpu.roll(yh, half, 1) * sp_ref[...])
            if scale != 1.0:
                yh = yh * scale
            for o_ref in outs:
                o_ref[:, cols] = yh.astype(o_ref.dtype)
            if vt is not None:
                for kb in range(y.shape[0] // vt_tk):
                    vt[h, kb] = yh[kb * vt_tk:(kb + 1) * vt_tk, :].T.astype(BF16)

    @pl.when(pl.program_id(0) == 0)
    def _():
        _cast_rows_to_bf16(wb_s, w_ref)
        emit(xs_ref, tabs_s, (os_ref,), None)

    emit(x_ref, tabs, out_refs, vt_ref)


def _proj_rope(x, xs, w, layer, col0, n_cols, tables, tables_s, pos_blocks, out_dtypes, *, scale, tm, vt=None):
    m_rows, d_model = x.shape
    ms = xs.shape[0]
    assert m_rows % tm == 0 and col0 % n_cols == 0 and n_cols % HEAD_W == 0
    rope = tables is not None
    in_specs = [pl.BlockSpec((tm, d_model), lambda i: (i, 0)),
                pl.BlockSpec((ms, d_model), lambda i: (0, 0)),
                pl.BlockSpec((None, d_model, n_cols), lambda i: (layer, 0, col0 // n_cols),
                             pipeline_mode=pl.Buffered(1))]
    args = [x, xs, w]
    if rope:
        in_specs += [pl.BlockSpec((tm, HEAD_W), lambda i: (i % pos_blocks, 0))] * 3
        in_specs += [pl.BlockSpec((ms, HEAD_W), lambda i: (0, 0))] * 3
        args += list(tables) + list(tables_s)
    out_specs = [pl.BlockSpec((tm, n_cols), lambda i: (i, 0)) for _ in out_dtypes]
    out_shape = [jax.ShapeDtypeStruct((m_rows, n_cols), dt) for dt in out_dtypes]
    out_specs.append(pl.BlockSpec((ms, n_cols), lambda i: (0, 0)))
    out_shape.append(jax.ShapeDtypeStruct((ms, n_cols), F32))
    vt_tk = 0
    if vt is not None:
        seq, vt_tk = vt
        assert seq % tm == 0 and tm % vt_tk == 0
        tiles_per_seq, n_heads = seq // tm, n_cols // HEAD_W
        out_specs.append(pl.BlockSpec((n_heads, tm // vt_tk, HEAD_W, vt_tk),
                                      lambda i: (i // tiles_per_seq, i % tiles_per_seq, 0, 0)))
        out_shape.append(jax.ShapeDtypeStruct(((m_rows // seq) * n_heads, seq // vt_tk, HEAD_W, vt_tk), BF16))
    kern = functools.partial(_proj_rope_kernel, rope=rope, scale=scale, n_out=len(out_dtypes), vt_tk=vt_tk)
    return pl.pallas_call(
        kern,
        grid=(m_rows // tm,),
        in_specs=in_specs,
        out_specs=out_specs,
        out_shape=out_shape,
        scratch_shapes=[pltpu.VMEM((d_model, n_cols), BF16)],
        compiler_params=_params("arbitrary"),
    )(*args)


def _diff_lambda(lam_ref, lam_init):
    lv = lam_ref[...]
    e1 = jnp.exp(jnp.sum(lv[0:1, :] * lv[1:2, :], axis=-1, keepdims=True))
    e2 = jnp.exp(jnp.sum(lv[2:3, :] * lv[3:4, :], axis=-1, keepdims=True))
    return e1 - e2 + lam_init


def _map_masks(rows):
    lane = lax.broadcasted_iota(jnp.int32, (rows, HEAD_W), 1)
    return lane < HEAD_DIM, lane >= HEAD_DIM


def _sub_ln(o, g_col, lam_init):
    return o * lax.rsqrt(jnp.mean(o * o, axis=0, keepdims=True) + LN_EPS) * g_col * (1.0 - lam_init)


def _attn_prompt_kernel(q_ref, k_ref, vt_ref, lam_ref, g_ref, o_ref, q2_s, m_s, l_s, acc_s, c_s, mx_s, s_s, p_s,
                        *, tq, lam_init):
    gw = min(MXU_COLS, tq)
    n_g = 2 * tq // gw
    m0, m1 = _map_masks(tq)
    lam = _diff_lambda(lam_ref, lam_init)

    def block(ki, diagonal, buf):
        kb = k_ref[ki * tq:(ki + 1) * tq, :]
        vtb = vt_ref[ki]
        n_keys = [min(tq, (g * gw) % tq + gw) if diagonal else tq for g in range(n_g)]
        for g in range(n_g):
            cols, nk = slice(g * gw, (g + 1) * gw), n_keys[g]
            s = lax.dot_general(kb[:nk], q2_s[cols, :], (((1,), (1,)), ((), ())), preferred_element_type=F32)
            if diagonal:
                key = lax.broadcasted_iota(jnp.int32, (nk, gw), 0)
                qry = lax.broadcasted_iota(jnp.int32, (nk, gw), 1) + (g * gw) % tq
                s = jnp.where(key <= qry, s, -jnp.inf)
            s_s[buf, g, 0:nk, :] = s
            mx_s[:, cols] = jnp.max(s, axis=0, keepdims=True)
        for g in range(n_g):
            cols, nk = slice(g * gw, (g + 1) * gw), n_keys[g]
            m_old = m_s[:, cols]
            m_new = jnp.maximum(m_old, mx_s[:, cols])
            c_s[:, cols] = jnp.exp(m_old - m_new)
            p_sum = jnp.zeros((1, gw), F32)
            for r in range(0, nk, SOFTMAX_ROWS):
                p = jnp.exp(s_s[buf, g, r:r + SOFTMAX_ROWS, :] - m_new)
                p_sum = p_sum + jnp.sum(p, axis=0, keepdims=True)
                p_s[buf, g, r:r + SOFTMAX_ROWS, :] = p.astype(BF16)
            l_s[:, cols] = c_s[:, cols] * l_s[:, cols] + p_sum
            m_s[:, cols] = m_new
        for g in range(n_g):
            cols, nk = slice(g * gw, (g + 1) * gw), n_keys[g]
            acc_s[:, cols] = c_s[:, cols] * acc_s[:, cols] + jnp.dot(vtb[:, :nk], p_s[buf, g, 0:nk, :],
                                                                     preferred_element_type=F32)

    n_blocks = 0
    for qi in range(q_ref.shape[0] // tq):
        rows = slice(qi * tq, (qi + 1) * tq)
        q = q_ref[rows, :]
        zero = jnp.zeros_like(q)
        q2_s[0:tq, :] = jnp.where(m0, q, zero)
        q2_s[tq:2 * tq, :] = jnp.where(m1, q, zero)
        m_s[...] = jnp.full_like(m_s, -jnp.inf)
        l_s[...] = jnp.zeros_like(l_s)
        acc_s[...] = jnp.zeros_like(acc_s)
        for ki in range(qi + 1):
            block(ki, ki == qi, n_blocks % 2)
            n_blocks += 1
        o = acc_s[...] / l_s[...]
        o = o[:, :tq] - lam * o[:, tq:]
        o_ref[rows, :] = _sub_ln(o, g_ref[...], lam_init).T.astype(o_ref.dtype)


def _attn_prompt(qb, kb, vt, lamv, subln_col, n_batch, seq, *, lam_init, tq):
    m_rows, d_model = qb.shape
    n_heads = d_model // HEAD_W
    assert seq % tq == 0 and tq % SOFTMAX_ROWS == 0 and vt.shape[1:] == (seq // tq, HEAD_W, tq)
    nq = seq // tq
    gw = min(MXU_COLS, tq)
    kern = functools.partial(_attn_prompt_kernel, tq=tq, lam_init=lam_init)
    seq_spec = pl.BlockSpec((seq, HEAD_W), lambda b, h: (b, h))
    return pl.pallas_call(
        kern,
        grid=(n_batch, n_heads),
        in_specs=[
            seq_spec,
            seq_spec,
            pl.BlockSpec((None, nq, HEAD_W, tq), lambda b, h: (b * n_heads + h, 0, 0, 0)),
            pl.BlockSpec((4, HEAD_DIM), lambda b, h: (0, 0)),
            pl.BlockSpec((HEAD_W, 1), lambda b, h: (0, 0)),
        ],
        out_specs=seq_spec,
        out_shape=jax.ShapeDtypeStruct((m_rows, d_model), BF16),
        scratch_shapes=[
            pltpu.VMEM((2 * tq, HEAD_W), BF16),
            pltpu.VMEM((1, 2 * tq), F32),
            pltpu.VMEM((1, 2 * tq), F32),
            pltpu.VMEM((HEAD_W, 2 * tq), F32),
            pltpu.VMEM((1, 2 * tq), F32),
            pltpu.VMEM((1, 2 * tq), F32),
            pltpu.VMEM((2, 2 * tq // gw, tq, gw), F32),
            pltpu.VMEM((2, 2 * tq // gw, tq, gw), BF16),
        ],
        compiler_params=_params("arbitrary", "arbitrary"),
    )(qb, kb, vt, lamv, subln_col)


def _attn_sample_kernel(pt_ref, q_ref, *refs, n_heads, n_slots, dec_seq, lam_init):
    del pt_ref
    k_pages = refs[:n_slots]
    v_pages = refs[n_slots:2 * n_slots]
    kn_ref, vn_ref, lam_ref, g_ref, o_ref, qm_s, m_s, l_s, acc_s = refs[2 * n_slots:]
    j = pl.program_id(1)
    hp = min(n_heads, SUBLANES)
    n_groups = n_heads // hp
    cw = 2 * dec_seq
    width = hp * cw

    @pl.when(j == 0)
    def _():
        m0, m1 = _map_masks(dec_seq)
        for h in range(n_heads):
            grp, hl = divmod(h, hp)
            qh = q_ref[:, h * HEAD_W:(h + 1) * HEAD_W]
            qm_s[grp, hl * cw:(hl + 1) * cw, :] = jnp.concatenate(
                [jnp.where(m0, qh, 0.0), jnp.where(m1, qh, 0.0)], axis=0).astype(BF16)
        m_s[...] = jnp.full_like(m_s, -jnp.inf)
        l_s[...] = jnp.zeros_like(l_s)
        acc_s[...] = jnp.zeros_like(acc_s)

    own = (lax.broadcasted_iota(jnp.int32, (hp, width), 1) // cw
           == lax.broadcasted_iota(jnp.int32, (hp, width), 0))

    def update(slot, k3_ref, v3_ref, causal):
        n_keys = k3_ref.shape[0]
        rows = n_keys * hp
        for grp in range(n_groups):
            heads = slice(grp * hp, (grp + 1) * hp)
            kr = k3_ref[:, heads, :].reshape(rows, HEAD_W).astype(BF16)
            u = lax.dot_general(kr, qm_s[grp], (((1,), (1,)), ((), ())), preferred_element_type=F32)
            u = u.reshape(n_keys, hp, width)
            if causal:
                key = lax.broadcasted_iota(jnp.int32, (n_keys, hp, width), 0)
                qry = lax.broadcasted_iota(jnp.int32, (n_keys, hp, width), 2) % dec_seq
                u = jnp.where(key <= qry, u, -jnp.inf)
            m_old = m_s[slot, grp]
            m_new = jnp.maximum(m_old, jnp.max(u, axis=0))
            p = jnp.where(own, jnp.exp(u - m_new), 0.0)
            corr = jnp.exp(m_old - m_new)
            l_s[slot, grp] = corr * l_s[slot, grp] + jnp.sum(p, axis=0)
            m_s[slot, grp] = m_new
            vr = v3_ref[:, heads, :].reshape(rows, HEAD_W).astype(BF16)
            pv = lax.dot_general(vr, p.reshape(rows, width).astype(BF16), (((0,), (0,)), ((), ())),
                                 preferred_element_type=F32)
            corr_col = jnp.sum(jnp.where(own, corr, 0.0), axis=0, keepdims=True)
            acc_s[slot, grp] = acc_s[slot, grp] * corr_col + pv

    for slot in range(n_slots):
        update(slot, k_pages[slot], v_pages[slot], False)

    @pl.when(j == pl.num_programs(1) - 1)
    def _():
        update(0, kn_ref, vn_ref, True)
        lam = _diff_lambda(lam_ref, lam_init)
        for grp in range(n_groups):
            m_all = m_s[0, grp]
            for slot in range(1, n_slots):
                m_all = jnp.maximum(m_all, m_s[slot, grp])
            l_all = jnp.zeros((hp, width), F32)
            acc = jnp.zeros((HEAD_W, width), F32)
            for slot in range(n_slots):
                w = jnp.exp(m_s[slot, grp] - m_all)
                l_all = l_all + w * l_s[slot, grp]
                acc = acc + acc_s[slot, grp] * jnp.sum(jnp.where(own, w, 0.0), axis=0, keepdims=True)
            o = acc / jnp.sum(jnp.where(own, l_all, 0.0), axis=0, keepdims=True)
            o = o - lam * pltpu.roll(o, width - dec_seq, 1)
            ot = _sub_ln(o, g_ref[...], lam_init).T
            for hl in range(hp):
                h = grp * hp + hl
                o_ref[:, h * HEAD_W:(h + 1) * HEAD_W] = ot[hl * cw:hl * cw + dec_seq, :]


def _attn_sample(q, cache_k, cache_v, page_table, kn, vn, lamv, subln_col, layer, dec_seq, *,
                 lam_init, n_slots):
    m_rows, d_model = q.shape
    n_heads = d_model // HEAD_W
    n_dec, n_pages = page_table.shape
    page = cache_k.shape[2]
    hp = min(n_heads, SUBLANES)
    assert n_pages % n_slots == 0 and dec_seq % SUBLANES == 0 and n_heads % hp == 0
    width = hp * 2 * dec_seq
    n_groups = n_heads // hp

    def page_spec(slot):
        return pl.BlockSpec((None, None, page, n_heads, HEAD_W),
                            lambda b, j, pt: (layer, pt[b, j * n_slots + slot], 0, 0, 0))

    row_spec = pl.BlockSpec((dec_seq, d_model), lambda b, j, pt: (b, 0))
    new_spec = pl.BlockSpec((None, dec_seq, n_heads, HEAD_W), lambda b, j, pt: (b, 0, 0, 0))
    kern = functools.partial(_attn_sample_kernel, n_heads=n_heads, n_slots=n_slots, dec_seq=dec_seq,
                             lam_init=lam_init)
    grid_spec = pltpu.PrefetchScalarGridSpec(
        num_scalar_prefetch=1,
        grid=(n_dec, n_pages // n_slots),
        in_specs=([row_spec] + [page_spec(s) for s in range(n_slots)] * 2 + [new_spec, new_spec]
                  + [pl.BlockSpec((4, HEAD_DIM), lambda b, j, pt: (0, 0)),
                     pl.BlockSpec((HEAD_W, 1), lambda b, j, pt: (0, 0))]),
        out_specs=row_spec,
        scratch_shapes=[
            pltpu.VMEM((n_groups, width, HEAD_W), BF16),
            pltpu.VMEM((n_slots, n_groups, hp, width), F32),
            pltpu.VMEM((n_slots, n_groups, hp, width), F32),
            pltpu.VMEM((n_slots, n_groups, HEAD_W, width), F32),
        ],
    )
    new_rows = lambda a: a.reshape(n_dec, dec_seq, n_heads, HEAD_W)
    return pl.pallas_call(
        kern,
        grid_spec=grid_spec,
        out_shape=jax.ShapeDtypeStruct((m_rows, d_model), F32),
        compiler_params=_params("arbitrary", "arbitrary"),
    )(page_table, q, *([cache_k] * n_slots), *([cache_v] * n_slots), new_rows(kn), new_rows(vn), lamv, subln_col)


def _tile(n, target):
    t = min(n, target)
    assert n % t == 0
    return t


def kernel(x_prompt, x_sample, cache_k, cache_v, page_table, gm_w_in, gm_b_in, gm_ln_g, gm_ln_b, gm_w_s, gm_b_s,
           gm_w_out, gm_b_out, at_w_qkv, at_lambda_q1, at_lambda_k1, at_lambda_q2, at_lambda_k2, at_subln_g,
           at_w_out, ln_mix_g, ln_mix_b, ln_ffn_g, ln_ffn_b, ffn_w_in, ffn_w_out):
    n_batch, seq, d_model = x_prompt.shape
    n_dec, dec_seq, _ = x_sample.shape
    depth = ln_mix_g.shape[0]
    n_heads = d_model // HEAD_W
    n_pages = page_table.shape[1]
    past_len = n_pages * cache_k.shape[2]
    alpha = (2 * depth) ** 0.25
    inner = gm_w_in.shape[2] // 2
    d_ff = ffn_w_out.shape[1]

    xp = x_prompt.reshape(n_batch * seq, d_model)
    xs = x_sample.reshape(n_dec * dec_seq, d_model)
    mp, ms = xp.shape[0], xs.shape[0]

    tm_p = _tile(seq, 1024)
    tm_r = _tile(seq, 512)
    tm_g = _tile(seq, 512)
    tn = _tile(inner, 512)
    tf = 256 if d_ff % 256 == 0 else d_ff

    tables_p = _rope_tables(jnp.arange(seq, dtype=jnp.int32))
    tables_s = _rope_tables(jnp.tile(past_len + jnp.arange(dec_seq, dtype=jnp.int32), n_dec))

    gm_v_p, gm_v_s, k_p, v_p, k_s, v_s = [], [], [], [], [], []
    for i in range(depth):
        j = i // 2
        if i % 2 == 0:
            bexp = jnp.repeat(gm_b_s[j].T, GROUP_W, axis=1)
            wexp = jnp.repeat(jnp.transpose(gm_w_s[j][:, :dec_seq, :dec_seq], (2, 1, 0)), GROUP_W, axis=2)
            gated_p, gv_p = _gmlp_in_prompt(xp, gm_w_in, gm_b_in, gm_ln_g, gm_ln_b, gm_w_s, bexp, j, seq,
                                            tm=tm_g, tn=tn)
            gated_s, gv_s = _gmlp_in_sample(xs, gm_w_in, gm_b_in, gm_ln_g, gm_ln_b, wexp, bexp[:dec_seq], j,
                                            dec_seq, tn=tn)
            gm_v_p.append(gv_p)
            gm_v_s.append(gv_s.reshape(n_dec, dec_seq, inner))
            mix_p = dict(a=gated_p, w=gm_w_out, bias=gm_b_out)
            mix_s = dict(a=gated_s, w=gm_w_out, bias=gm_b_out)
        else:
            lam_init = 0.8 - 0.6 * math.exp(-0.3 * i)
            lamv = jnp.stack([at_lambda_q1[j], at_lambda_k1[j], at_lambda_q2[j], at_lambda_k2[j]])
            subln = at_subln_g[j].reshape(HEAD_W, 1)
            scale = HEAD_DIM ** -0.5
            tq = _tile(seq, 512)
            proj = functools.partial(_proj_rope, xp, xs, at_w_qkv, j, n_cols=d_model, tm=tm_r)
            qb, qs = proj(col0=0, tables=tables_p, tables_s=tables_s, pos_blocks=seq // tm_r,
                          out_dtypes=(BF16,), scale=scale)
            kf, kb, kn = proj(col0=d_model, tables=tables_p, tables_s=tables_s, pos_blocks=seq // tm_r,
                              out_dtypes=(F32, BF16), scale=1.0)
            vf, vn, vt = proj(col0=2 * d_model, tables=None, tables_s=None, pos_blocks=1,
                              out_dtypes=(F32,), scale=1.0, vt=(seq, tq))
            a_p = _attn_prompt(qb, kb, vt, lamv, subln, n_batch, seq, lam_init=lam_init, tq=tq)
            a_s = _attn_sample(qs, cache_k, cache_v, page_table, kn, vn, lamv, subln, j, dec_seq,
                               lam_init=lam_init, n_slots=math.gcd(n_pages, 8))
            k_p.append(kf.reshape(n_batch, seq, n_heads, HEAD_W))
            v_p.append(vf.reshape(n_batch, seq, n_heads, HEAD_W))
            k_s.append(kn.reshape(n_dec, dec_seq, n_heads, HEAD_W))
            v_s.append(vn.reshape(n_dec, dec_seq, n_heads, HEAD_W))
            mix_p = dict(a=a_p, w=at_w_out, bias=None)
            mix_s = dict(a=a_s, w=at_w_out, bias=None)
        xp, xs = _proj_ln(mix_p["a"], mix_s["a"], mix_p["w"], j, mix_p["bias"], xp, xs, ln_mix_g, ln_mix_b, i,
                          alpha=alpha, tm=tm_r)
        xp, xs = _ffn(xp, xs, ffn_w_in, ffn_w_out, i, ln_ffn_g, ln_ffn_b, alpha=alpha, tm=tm_p, tf=tf)

    return (xp.reshape(n_batch, seq, d_model), xs.reshape(n_dec, dec_seq, d_model),
            jnp.stack(gm_v_p), jnp.stack(gm_v_s), jnp.stack(k_p), jnp.stack(v_p), jnp.stack(k_s), jnp.stack(v_s))
```

```python
import functools
import math

import jax
import jax.numpy as jnp
from jax import lax
from jax.experimental import pallas as pl
from jax.experimental.pallas import tpu as pltpu

F32 = jnp.float32
BF16 = jnp.bfloat16

LN_EPS = 1e-5
CHUNK = 128
GROUP_W = 128
HEAD_DIM = 64
HEAD_W = 2 * HEAD_DIM
ROT_DIM = HEAD_DIM // 4
ROPE_THETA = 500000.0
LANES = 128
SUBLANES = 8
MXU_COLS = 256
DENOM_ROWS = 16
LN_ROWS = 128
SOFTMAX_ROWS = 64
VMEM_LIMIT_BYTES = 60 * 1024 * 1024


def _params(*sem):
    return pltpu.CompilerParams(dimension_semantics=sem, vmem_limit_bytes=VMEM_LIMIT_BYTES)


def _layer_norm(y, g, b):
    mu = jnp.mean(y, axis=-1, keepdims=True)
    yc = y - mu
    var = jnp.mean(yc * yc, axis=-1, keepdims=True)
    return yc * lax.rsqrt(var + LN_EPS) * g + b


def _residual_ln_inplace(o_ref, x_ref, bias_ref, g_ref, b_ref, alpha, tm):
    block = min(tm, LN_ROWS)
    assert tm % block == 0

    def body(r, carry):
        rows = pl.ds(pl.multiple_of(r * block, block), block)
        y = alpha * x_ref[rows, :] + o_ref[rows, :]
        if bias_ref is not None:
            y = y + bias_ref[...]
        o_ref[rows, :] = _layer_norm(y, g_ref[...], b_ref[...])
        return carry

    lax.fori_loop(0, tm // block, body, 0)


def _cast_rows_to_bf16(dst_ref, src_ref):
    rows = 256
    n = src_ref.shape[0]
    assert n % rows == 0

    def body(r, carry):
        sl = pl.ds(pl.multiple_of(r * rows, rows), rows)
        dst_ref[sl, :] = src_ref[sl, :].astype(BF16)
        return carry

    lax.fori_loop(0, n // rows, body, 0)


def _gmlp_project(x_ref, w_ref, b_ref, xb_s, z_s, wb_s=None):
    j = pl.program_id(1)

    @pl.when(j == 0)
    def _():
        xb_s[...] = x_ref[...].astype(BF16)

    if wb_s is None:
        wb = w_ref[...].astype(BF16)
    else:
        @pl.when(pl.program_id(0) == 0)
        def _():
            wb_s[j] = w_ref[...].astype(BF16)

        wb = wb_s[j]
    z = jnp.dot(xb_s[...], wb, preferred_element_type=F32) + b_ref[...]
    z_s[j] = 0.5 * z * (1.0 + lax.erf(z * math.sqrt(0.5)))


def _gmlp_v_layer_norm(z_s, lng_ref, lnb_ref, rows, n_half, tn, inner):
    vs = [z_s[n_half + c, rows, :] for c in range(n_half)]
    mu = sum(jnp.sum(v, axis=-1, keepdims=True) for v in vs) / inner
    var = sum(jnp.sum(jnp.square(v - mu), axis=-1, keepdims=True) for v in vs) / inner
    rstd = lax.rsqrt(var + LN_EPS)
    return [(vs[c] - mu) * rstd * lng_ref[:, c * tn:(c + 1) * tn] + lnb_ref[:, c * tn:(c + 1) * tn]
            for c in range(n_half)]


def _gmlp_in_prompt_kernel(x_ref, w_ref, b_ref, lng_ref, lnb_ref, ws_ref, bexp_ref,
                           gated_ref, gmv_ref, xb_s, z_s, vn_s, wb_s, *, tm, tn, inner):
    _gmlp_project(x_ref, w_ref, b_ref, xb_s, z_s, wb_s)
    n_half = inner // tn
    n_groups = inner // GROUP_W
    n_rc = tm // CHUNK
    per_chunk = tn // GROUP_W

    @pl.when(pl.program_id(1) == pl.num_programs(1) - 1)
    def _():
        for r in range(n_rc):
            rows = slice(r * CHUNK, (r + 1) * CHUNK)
            vn = _gmlp_v_layer_norm(z_s, lng_ref, lnb_ref, rows, n_half, tn, inner)
            for c in range(n_half):
                if r == n_rc - 1:
                    gmv_ref[:, c * tn:(c + 1) * tn] = vn[c]
                for q in range(per_chunk):
                    g = c * per_chunk + q
                    vn_s[g, :, r * CHUNK:(r + 1) * CHUNK] = vn[c][:, q * GROUP_W:(q + 1) * GROUP_W].astype(BF16)
        t_idx = lax.broadcasted_iota(jnp.int32, (CHUNK, CHUNK), 0)
        s_idx = lax.broadcasted_iota(jnp.int32, (CHUNK, CHUNK), 1)
        causal = s_idx <= t_idx
        for g in range(n_groups):
            w_causal = jnp.where(causal, ws_ref[g], 0.0).astype(BF16)
            mixed = jnp.dot(w_causal, vn_s[g], preferred_element_type=F32)
            cols = slice(g * GROUP_W, (g + 1) * GROUP_W)
            c, q = divmod(g, per_chunk)
            for r in range(n_rc):
                rows = slice(r * CHUNK, (r + 1) * CHUNK)
                u = z_s[c, rows, q * GROUP_W:(q + 1) * GROUP_W]
                m = mixed[:, r * CHUNK:(r + 1) * CHUNK] + bexp_ref[:, cols]
                gated_ref[rows, cols] = (u * m).astype(BF16)


def _gmlp_in_prompt(x, w_in, b_in, ln_g, ln_b, w_s, bexp, layer, seq, *, tm, tn):
    m_rows, d_model = x.shape
    inner = w_in.shape[2] // 2
    n_groups = inner // GROUP_W
    n_batch = m_rows // seq
    assert seq % tm == 0 and tm % CHUNK == 0 and inner % tn == 0 and tn % GROUP_W == 0
    tiles_per_seq = seq // tm
    nj = 2 * inner // tn
    kern = functools.partial(_gmlp_in_prompt_kernel, tm=tm, tn=tn, inner=inner)
    return pl.pallas_call(
        kern,
        grid=(m_rows // tm, nj),
        in_specs=[
            pl.BlockSpec((tm, d_model), lambda i, j: (i, 0)),
            pl.BlockSpec((None, d_model, tn), lambda i, j: (layer, 0, jnp.where(i == 0, j, nj - 1))),
            pl.BlockSpec((None, 1, tn), lambda i, j: (layer, 0, j)),
            pl.BlockSpec((None, 1, inner), lambda i, j: (layer, 0, 0)),
            pl.BlockSpec((None, 1, inner), lambda i, j: (layer, 0, 0)),
            pl.BlockSpec((None, n_groups, CHUNK, CHUNK), lambda i, j: (layer, 0, 0, 0)),
            pl.BlockSpec((CHUNK, inner), lambda i, j: (0, 0)),
        ],
        out_specs=[
            pl.BlockSpec((tm, inner), lambda i, j: (i, 0)),
            pl.BlockSpec((None, CHUNK, inner), lambda i, j: (i // tiles_per_seq, 0, 0)),
        ],
        out_shape=[
            jax.ShapeDtypeStruct((m_rows, inner), BF16),
            jax.ShapeDtypeStruct((n_batch, CHUNK, inner), F32),
        ],
        scratch_shapes=[
            pltpu.VMEM((tm, d_model), BF16),
            pltpu.VMEM((nj, tm, tn), F32),
            pltpu.VMEM((n_groups, CHUNK, tm), BF16),
            pltpu.VMEM((nj, d_model, tn), BF16),
        ],
        compiler_params=_params("arbitrary", "arbitrary"),
    )(x, w_in, b_in.reshape(b_in.shape[0], 1, -1), ln_g.reshape(ln_g.shape[0], 1, -1),
      ln_b.reshape(ln_b.shape[0], 1, -1), w_s, bexp)


def _gmlp_in_sample_kernel(x_ref, w_ref, b_ref, lng_ref, lnb_ref, wexp_ref, bexp_ref,
                           gated_ref, gmv_ref, xb_s, z_s, *, tm, tn, inner, dec_seq):
    _gmlp_project(x_ref, w_ref, b_ref, xb_s, z_s)
    n_half = inner // tn

    @pl.when(pl.program_id(1) == pl.num_programs(1) - 1)
    def _():
        vn = _gmlp_v_layer_norm(z_s, lng_ref, lnb_ref, slice(0, tm), n_half, tn, inner)
        t_idx = lax.broadcasted_iota(jnp.int32, (dec_seq, tn), 0)
        for c in range(n_half):
            cols = slice(c * tn, (c + 1) * tn)
            gmv_ref[:, cols] = vn[c]
            for b in range(tm // dec_seq):
                rows = slice(b * dec_seq, (b + 1) * dec_seq)
                vb = vn[c][rows, :]
                mixed = bexp_ref[:, cols]
                for s in range(dec_seq):
                    w_ts = jnp.where(t_idx >= s, wexp_ref[s, :, cols], 0.0)
                    mixed = mixed + w_ts * vb[s:s + 1, :]
                gated_ref[rows, cols] = (z_s[c, rows, :] * mixed).astype(BF16)


def _gmlp_in_sample(x, w_in, b_in, ln_g, ln_b, wexp, bexp, layer, dec_seq, *, tn):
    m_rows, d_model = x.shape
    inner = w_in.shape[2] // 2
    assert dec_seq % 8 == 0 and dec_seq <= CHUNK and inner % tn == 0
    nj = 2 * inner // tn
    kern = functools.partial(_gmlp_in_sample_kernel, tm=m_rows, tn=tn, inner=inner, dec_seq=dec_seq)
    return pl.pallas_call(
        kern,
        grid=(1, nj),
        in_specs=[
            pl.BlockSpec((m_rows, d_model), lambda i, j: (0, 0)),
            pl.BlockSpec((None, d_model, tn), lambda i, j: (layer, 0, j)),
            pl.BlockSpec((None, 1, tn), lambda i, j: (layer, 0, j)),
            pl.BlockSpec((None, 1, inner), lambda i, j: (layer, 0, 0)),
            pl.BlockSpec((None, 1, inner), lambda i, j: (layer, 0, 0)),
            pl.BlockSpec((dec_seq, dec_seq, inner), lambda i, j: (0, 0, 0)),
            pl.BlockSpec((dec_seq, inner), lambda i, j: (0, 0)),
        ],
        out_specs=[
            pl.BlockSpec((m_rows, inner), lambda i, j: (0, 0)),
            pl.BlockSpec((m_rows, inner), lambda i, j: (0, 0)),
        ],
        out_shape=[
            jax.ShapeDtypeStruct((m_rows, inner), BF16),
            jax.ShapeDtypeStruct((m_rows, inner), F32),
        ],
        scratch_shapes=[
            pltpu.VMEM((m_rows, d_model), BF16),
            pltpu.VMEM((nj, m_rows, tn), F32),
        ],
        compiler_params=_params("arbitrary", "arbitrary"),
    )(x, w_in, b_in.reshape(b_in.shape[0], 1, -1), ln_g.reshape(ln_g.shape[0], 1, -1),
      ln_b.reshape(ln_b.shape[0], 1, -1), wexp, bexp)


def _proj_ln_kernel(*refs, alpha, has_bias):
    if has_bias:
        a_ref, as_ref, w_ref, bias_ref, x_ref, xs_ref, g_ref, b_ref, o_ref, os_ref, wb_s = refs
    else:
        a_ref, as_ref, w_ref, x_ref, xs_ref, g_ref, b_ref, o_ref, os_ref, wb_s = refs
        bias_ref = None

    def apply(a, x_in, o):
        o[...] = jnp.dot(a[...].astype(BF16), wb_s[...], preferred_element_type=F32)
        _residual_ln_inplace(o, x_in, bias_ref, g_ref, b_ref, alpha, o.shape[0])

    @pl.when(pl.program_id(0) == 0)
    def _():
        _cast_rows_to_bf16(wb_s, w_ref)
        apply(as_ref, xs_ref, os_ref)

    apply(a_ref, x_ref, o_ref)


def _proj_ln(a, a_s, w, layer, bias, x, xs, ln_g, ln_b, ln_idx, *, alpha, tm):
    m_rows, k_dim = a.shape
    ms = a_s.shape[0]
    d_model = x.shape[1]
    assert m_rows % tm == 0
    vec = lambda idx: pl.BlockSpec((None, 1, d_model), lambda i: (idx, 0, 0))
    in_specs = [pl.BlockSpec((tm, k_dim), lambda i: (i, 0)),
                pl.BlockSpec((ms, k_dim), lambda i: (0, 0)),
                pl.BlockSpec((None, k_dim, d_model), lambda i: (layer, 0, 0), pipeline_mode=pl.Buffered(1))]
    args = [a, a_s, w]
    if bias is not None:
        in_specs.append(vec(layer))
        args.append(bias.reshape(bias.shape[0], 1, -1))
    in_specs += [pl.BlockSpec((tm, d_model), lambda i: (i, 0)), pl.BlockSpec((ms, d_model), lambda i: (0, 0)),
                 vec(ln_idx), vec(ln_idx)]
    args += [x, xs, ln_g.reshape(ln_g.shape[0], 1, -1), ln_b.reshape(ln_b.shape[0], 1, -1)]
    kern = functools.partial(_proj_ln_kernel, alpha=alpha, has_bias=bias is not None)
    return pl.pallas_call(
        kern,
        grid=(m_rows // tm,),
        in_specs=in_specs,
        out_specs=[pl.BlockSpec((tm, d_model), lambda i: (i, 0)), pl.BlockSpec((ms, d_model), lambda i: (0, 0))],
        out_shape=[jax.ShapeDtypeStruct((m_rows, d_model), F32), jax.ShapeDtypeStruct((ms, d_model), F32)],
        scratch_shapes=[pltpu.VMEM((k_dim, d_model), BF16)],
        compiler_params=_params("arbitrary"),
    )(*args)


def _ffn_kernel(x_ref, xs_ref, wg_ref, wu_ref, wo_ref, g_ref, b_ref, o_ref, os_ref, xb_s, xsb_s, *, alpha):
    i, f = pl.program_id(0), pl.program_id(1)
    last = pl.num_programs(1) - 1

    def start(x_in, xb, o):
        @pl.when(f == 0)
        def _():
            xb[...] = x_in[...].astype(BF16)
            o[...] = jnp.zeros_like(o)

    def accumulate(xb, o, wg, wu, wo):
        gate = jnp.dot(xb[...], wg, preferred_element_type=F32)
        up = jnp.dot(xb[...], wu, preferred_element_type=F32)
        h = (jax.nn.silu(gate) * up).astype(BF16)
        o[...] += jnp.dot(h, wo, preferred_element_type=F32)

    def finish(x_in, o):
        @pl.when(f == last)
        def _():
            _residual_ln_inplace(o, x_in, None, g_ref, b_ref, alpha, o.shape[0])

    start(x_ref, xb_s, o_ref)
    wg = wg_ref[...].astype(BF16)
    wu = wu_ref[...].astype(BF16)
    wo = wo_ref[...].astype(BF16)
    accumulate(xb_s, o_ref, wg, wu, wo)
    finish(x_ref, o_ref)

    @pl.when(i == 0)
    def _():
        start(xs_ref, xsb_s, os_ref)
        accumulate(xsb_s, os_ref, wg, wu, wo)
        finish(xs_ref, os_ref)


def _ffn(x, xs, w_in, w_out, layer, ln_g, ln_b, *, alpha, tm, tf):
    m_rows, d_model = x.shape
    ms = xs.shape[0]
    d_ff = w_out.shape[1]
    assert m_rows % tm == 0 and d_ff % tf == 0
    nf = d_ff // tf
    vec = pl.BlockSpec((None, 1, d_model), lambda i, f: (layer, 0, 0))
    kern = functools.partial(_ffn_kernel, alpha=alpha)
    return pl.pallas_call(
        kern,
        grid=(m_rows // tm, nf),
        in_specs=[
            pl.BlockSpec((tm, d_model), lambda i, f: (i, 0)),
            pl.BlockSpec((ms, d_model), lambda i, f: (0, 0)),
            pl.BlockSpec((None, d_model, tf), lambda i, f: (layer, 0, f)),
            pl.BlockSpec((None, d_model, tf), lambda i, f: (layer, 0, nf + f)),
            pl.BlockSpec((None, tf, d_model), lambda i, f: (layer, f, 0)),
            vec, vec,
        ],
        out_specs=[pl.BlockSpec((tm, d_model), lambda i, f: (i, 0)),
                   pl.BlockSpec((ms, d_model), lambda i, f: (0, 0))],
        out_shape=[jax.ShapeDtypeStruct((m_rows, d_model), F32), jax.ShapeDtypeStruct((ms, d_model), F32)],
        scratch_shapes=[pltpu.VMEM((tm, d_model), BF16), pltpu.VMEM((ms, d_model), BF16)],
        compiler_params=_params("arbitrary", "arbitrary"),
    )(x, xs, w_in, w_in, w_out, ln_g.reshape(ln_g.shape[0], 1, -1), ln_b.reshape(ln_b.shape[0], 1, -1))


def _rope_tables(pos):
    half = ROT_DIM // 2
    inv = jnp.power(ROPE_THETA, -jnp.arange(half, dtype=F32) * 2.0 / ROT_DIM)
    ang = pos.astype(F32)[:, None] * inv[None, :]
    cos, sin = jnp.cos(ang), jnp.sin(ang)
    n = pos.shape[0]
    rest = HEAD_DIM - ROT_DIM
    c = jnp.concatenate([cos, cos, jnp.ones((n, rest), F32)], axis=1)
    s_next = jnp.concatenate([-sin, jnp.zeros((n, half + rest), F32)], axis=1)
    s_prev = jnp.concatenate([jnp.zeros((n, half), F32), sin, jnp.zeros((n, rest), F32)], axis=1)
    return tuple(jnp.tile(t, (1, HEAD_W // HEAD_DIM)) for t in (c, s_next, s_prev))


def _proj_rope_kernel(*refs, rope, scale, n_out, vt_tk):
    n_tab = 3 if rope else 0
    x_ref, xs_ref, w_ref = refs[:3]
    tabs, tabs_s = refs[3:3 + n_tab], refs[3 + n_tab:3 + 2 * n_tab]
    k = 3 + 2 * n_tab
    out_refs, os_ref = refs[k:k + n_out], refs[k + n_out]
    vt_ref = refs[k + n_out + 1] if vt_tk else None
    wb_s = refs[-1]

    def emit(x_in, tables, outs, vt):
        y = jnp.dot(x_in[...].astype(BF16), wb_s[...], preferred_element_type=F32)
        for h in range(y.shape[1] // HEAD_W):
            cols = slice(h * HEAD_W, (h + 1) * HEAD_W)
            yh = y[:, cols]
            if rope:
                c_ref, sn_ref, sp_ref = tables
                half = ROT_DIM // 2
                yh = (yh * c_ref[...] + pltpu.roll(yh, HEAD_W - half, 1) * sn_ref[...]
                      + pltpu.roll(yh, half, 1) * sp_ref[...])
            if scale != 1.0:
                yh = yh * scale
            for o_ref in outs:
                o_ref[:, cols] = yh.astype(o_ref.dtype)
            if vt is not None:
                for kb in range(y.shape[0] // vt_tk):
                    vt[h, kb] = yh[kb * vt_tk:(kb + 1) * vt_tk, :].T.astype(BF16)

    @pl.when(pl.program_id(0) == 0)
    def _():
        _cast_rows_to_bf16(wb_s, w_ref)
        emit(xs_ref, tabs_s, (os_ref,), None)

    emit(x_ref, tabs, out_refs, vt_ref)


def _proj_rope(x, xs, w, layer, col0, n_cols, tables, tables_s, pos_blocks, out_dtypes, *, scale, tm, vt=None):
    m_rows, d_model = x.shape
    ms = xs.shape[0]
    assert m_rows % tm == 0 and col0 % n_cols == 0 and n_cols % HEAD_W == 0
    rope = tables is not None
    in_specs = [pl.BlockSpec((tm, d_model), lambda i: (i, 0)),
                pl.BlockSpec((ms, d_model), lambda i: (0, 0)),
                pl.BlockSpec((None, d_model, n_cols), lambda i: (layer, 0, col0 // n_cols),
                             pipeline_mode=pl.Buffered(1))]
    args = [x, xs, w]
    if rope:
        in_specs += [pl.BlockSpec((tm, HEAD_W), lambda i: (i % pos_blocks, 0))] * 3
        in_specs += [pl.BlockSpec((ms, HEAD_W), lambda i: (0, 0))] * 3
        args += list(tables) + list(tables_s)
    out_specs = [pl.BlockSpec((tm, n_cols), lambda i: (i, 0)) for _ in out_dtypes]
    out_shape = [jax.ShapeDtypeStruct((m_rows, n_cols), dt) for dt in out_dtypes]
    out_specs.append(pl.BlockSpec((ms, n_cols), lambda i: (0, 0)))
    out_shape.append(jax.ShapeDtypeStruct((ms, n_cols), F32))
    vt_tk = 0
    if vt is not None:
        seq, vt_tk = vt
        assert seq % tm == 0 and tm % vt_tk == 0
        tiles_per_seq, n_heads = seq // tm, n_cols // HEAD_W
        out_specs.append(pl.BlockSpec((n_heads, tm // vt_tk, HEAD_W, vt_tk),
                                      lambda i: (i // tiles_per_seq, i % tiles_per_seq, 0, 0)))
        out_shape.append(jax.ShapeDtypeStruct(((m_rows // seq) * n_heads, seq // vt_tk, HEAD_W, vt_tk), BF16))
    kern = functools.partial(_proj_rope_kernel, rope=rope, scale=scale, n_out=len(out_dtypes), vt_tk=vt_tk)
    return pl.pallas_call(
        kern,
        grid=(m_rows // tm,),
        in_specs=in_specs,
        out_specs=out_specs,
        out_shape=out_shape,
        scratch_shapes=[pltpu.VMEM((d_model, n_cols), BF16)],
        compiler_params=_params("arbitrary"),
    )(*args)


def _diff_lambda(lam_ref, lam_init):
    lv = lam_ref[...]
    e1 = jnp.exp(jnp.sum(lv[0:1, :] * lv[1:2, :], axis=-1, keepdims=True))
    e2 = jnp.exp(jnp.sum(lv[2:3, :] * lv[3:4, :], axis=-1, keepdims=True))
    return e1 - e2 + lam_init


def _map_masks(rows):
    lane = lax.broadcasted_iota(jnp.int32, (rows, HEAD_W), 1)
    return lane < HEAD_DIM, lane >= HEAD_DIM


def _sub_ln(o, g_col, lam_init):
    return o * lax.rsqrt(jnp.mean(o * o, axis=0, keepdims=True) + LN_EPS) * g_col * (1.0 - lam_init)


def _attn_prompt_kernel(q_ref, k_ref, vt_ref, lam_ref, g_ref, o_ref, q2_s, m_s, acc_s, c_s, mx_s, s_s, p_s,
                        *, tq, lam_init):
    gw = min(MXU_COLS, tq)
    n_g = 2 * tq // gw
    m0, m1 = _map_masks(tq)
    lam = _diff_lambda(lam_ref, lam_init)

    def block(ki, diagonal, buf):
        kb = k_ref[ki * tq:(ki + 1) * tq, :]
        vtb = jnp.concatenate([vt_ref[ki], jnp.ones((DENOM_ROWS, tq), BF16)], axis=0)
        n_keys = [min(tq, (g * gw) % tq + gw) if diagonal else tq for g in range(n_g)]
        for g in range(n_g):
            cols, nk = slice(g * gw, (g + 1) * gw), n_keys[g]
            s = lax.dot_general(kb[:nk], q2_s[cols, :], (((1,), (1,)), ((), ())), preferred_element_type=F32)
            if diagonal:
                key = lax.broadcasted_iota(jnp.int32, (nk, gw), 0)
                qry = lax.broadcasted_iota(jnp.int32, (nk, gw), 1) + (g * gw) % tq
                s = jnp.where(key <= qry, s, -jnp.inf)
            s_s[buf, g, 0:nk, :] = s
            mx_s[:, cols] = jnp.max(s, axis=0, keepdims=True)
        for g in range(n_g):
            cols, nk = slice(g * gw, (g + 1) * gw), n_keys[g]
            m_old = m_s[:, cols]
            m_new = jnp.maximum(m_old, mx_s[:, cols])
            c_s[:, cols] = jnp.exp2(m_old - m_new)
            for r in range(0, nk, SOFTMAX_ROWS):
                p_s[buf, g, r:r + SOFTMAX_ROWS, :] = jnp.exp2(s_s[buf, g, r:r + SOFTMAX_ROWS, :] - m_new).astype(BF16)
            m_s[:, cols] = m_new
        for g in range(n_g):
            cols, nk = slice(g * gw, (g + 1) * gw), n_keys[g]
            acc_s[:, cols] = c_s[:, cols] * acc_s[:, cols] + jnp.dot(vtb[:, :nk], p_s[buf, g, 0:nk, :],
                                                                     preferred_element_type=F32)

    n_blocks = 0
    for qi in range(q_ref.shape[0] // tq):
        rows = slice(qi * tq, (qi + 1) * tq)
        q = q_ref[rows, :]
        zero = jnp.zeros_like(q)
        q2_s[0:tq, :] = jnp.where(m0, q, zero)
        q2_s[tq:2 * tq, :] = jnp.where(m1, q, zero)
        m_s[...] = jnp.full_like(m_s, -jnp.inf)
        acc_s[...] = jnp.zeros_like(acc_s)
        for ki in range(qi + 1):
            block(ki, ki == qi, n_blocks % 2)
            n_blocks += 1
        o = acc_s[0:HEAD_W, :] / acc_s[HEAD_W:HEAD_W + 1, :]
        o = o[:, :tq] - lam * o[:, tq:]
        o_ref[rows, :] = _sub_ln(o, g_ref[...], lam_init).T.astype(o_ref.dtype)


def _attn_prompt(qb, kb, vt, lamv, subln_col, n_batch, seq, *, lam_init, tq):
    m_rows, d_model = qb.shape
    n_heads = d_model // HEAD_W
    assert seq % tq == 0 and tq % SOFTMAX_ROWS == 0 and vt.shape[1:] == (seq // tq, HEAD_W, tq)
    nq = seq // tq
    gw = min(MXU_COLS, tq)
    kern = functools.partial(_attn_prompt_kernel, tq=tq, lam_init=lam_init)
    seq_spec = pl.BlockSpec((seq, HEAD_W), lambda b, h: (b, h))
    return pl.pallas_call(
        kern,
        grid=(n_batch, n_heads),
        in_specs=[
            seq_spec,
            seq_spec,
            pl.BlockSpec((None, nq, HEAD_W, tq), lambda b, h: (b * n_heads + h, 0, 0, 0)),
            pl.BlockSpec((4, HEAD_DIM), lambda b, h: (0, 0)),
            pl.BlockSpec((HEAD_W, 1), lambda b, h: (0, 0)),
        ],
        out_specs=seq_spec,
        out_shape=jax.ShapeDtypeStruct((m_rows, d_model), BF16),
        scratch_shapes=[
            pltpu.VMEM((2 * tq, HEAD_W), BF16),
            pltpu.VMEM((1, 2 * tq), F32),
            pltpu.VMEM((HEAD_W + DENOM_ROWS, 2 * tq), F32),
            pltpu.VMEM((1, 2 * tq), F32),
            pltpu.VMEM((1, 2 * tq), F32),
            pltpu.VMEM((2, 2 * tq // gw, tq, gw), F32),
            pltpu.VMEM((2, 2 * tq // gw, tq, gw), BF16),
        ],
        compiler_params=_params("arbitrary", "arbitrary"),
    )(qb, kb, vt, lamv, subln_col)


def _attn_sample_kernel(pt_ref, q_ref, *refs, n_heads, n_slots, dec_seq, lam_init):
    del pt_ref
    k_pages = refs[:n_slots]
    v_pages = refs[n_slots:2 * n_slots]
    kn_ref, vn_ref, lam_ref, g_ref, o_ref, qm_s, m_s, l_s, acc_s = refs[2 * n_slots:]
    j = pl.program_id(1)
    hp = min(n_heads, SUBLANES)
    n_groups = n_heads // hp
    cw = 2 * dec_seq
    width = hp * cw

    @pl.when(j == 0)
    def _():
        m0, m1 = _map_masks(dec_seq)
        for h in range(n_heads):
            grp, hl = divmod(h, hp)
            qh = q_ref[:, h * HEAD_W:(h + 1) * HEAD_W]
            qm_s[grp, hl * cw:(hl + 1) * cw, :] = jnp.concatenate(
                [jnp.where(m0, qh, 0.0), jnp.where(m1, qh, 0.0)], axis=0).astype(BF16)
        m_s[...] = jnp.full_like(m_s, -jnp.inf)
        l_s[...] = jnp.zeros_like(l_s)
        acc_s[...] = jnp.zeros_like(acc_s)

    own = (lax.broadcasted_iota(jnp.int32, (hp, width), 1) // cw
           == lax.broadcasted_iota(jnp.int32, (hp, width), 0))

    def update(slot, k3_ref, v3_ref, causal):
        n_keys = k3_ref.shape[0]
        rows = n_keys * hp
        for grp in range(n_groups):
            heads = slice(grp * hp, (grp + 1) * hp)
            kr = k3_ref[:, heads, :].reshape(rows, HEAD_W).astype(BF16)
            u = lax.dot_general(kr, qm_s[grp], (((1,), (1,)), ((), ())), preferred_element_type=F32)
            u = u.reshape(n_keys, hp, width)
            if causal:
                key = lax.broadcasted_iota(jnp.int32, (n_keys, hp, width), 0)
                qry = lax.broadcasted_iota(jnp.int32, (n_keys, hp, width), 2) % dec_seq
                u = jnp.where(key <= qry, u, -jnp.inf)
            m_old = m_s[slot, grp]
            m_new = jnp.maximum(m_old, jnp.max(u, axis=0))
            p = jnp.where(own, jnp.exp2(u - m_new), 0.0)
            corr = jnp.exp2(m_old - m_new)
            l_s[slot, grp] = corr * l_s[slot, grp] + jnp.sum(p, axis=0)
            m_s[slot, grp] = m_new
            vr = v3_ref[:, heads, :].reshape(rows, HEAD_W).astype(BF16)
            pv = lax.dot_general(vr, p.reshape(rows, width).astype(BF16), (((0,), (0,)), ((), ())),
                                 preferred_element_type=F32)
            corr_col = jnp.sum(jnp.where(own, corr, 0.0), axis=0, keepdims=True)
            acc_s[slot, grp] = acc_s[slot, grp] * corr_col + pv

    for slot in range(n_slots):
        update(slot, k_pages[slot], v_pages[slot], False)

    @pl.when(j == pl.num_programs(1) - 1)
    def _():
        update(0, kn_ref, vn_ref, True)
        lam = _diff_lambda(lam_ref, lam_init)
        for grp in range(n_groups):
            m_all = m_s[0, grp]
            for slot in range(1, n_slots):
                m_all = jnp.maximum(m_all, m_s[slot, grp])
            l_all = jnp.zeros((hp, width), F32)
            acc = jnp.zeros((HEAD_W, width), F32)
            for slot in range(n_slots):
                w = jnp.exp2(m_s[slot, grp] - m_all)
                l_all = l_all + w * l_s[slot, grp]
                acc = acc + acc_s[slot, grp] * jnp.sum(jnp.where(own, w, 0.0), axis=0, keepdims=True)
            o = acc / jnp.sum(jnp.where(own, l_all, 0.0), axis=0, keepdims=True)
            o = o - lam * pltpu.roll(o, width - dec_seq, 1)
            ot = _sub_ln(o, g_ref[...], lam_init).T
            for hl in range(hp):
                h = grp * hp + hl
                o_ref[:, h * HEAD_W:(h + 1) * HEAD_W] = ot[hl * cw:hl * cw + dec_seq, :]


def _attn_sample(q, cache_k, cache_v, page_table, kn, vn, lamv, subln_col, layer, dec_seq, *,
                 lam_init, n_slots):
    m_rows, d_model = q.shape
    n_heads = d_model // HEAD_W
    n_dec, n_pages = page_table.shape
    page = cache_k.shape[2]
    hp = min(n_heads, SUBLANES)
    assert n_pages % n_slots == 0 and dec_seq % SUBLANES == 0 and n_heads % hp == 0
    width = hp * 2 * dec_seq
    n_groups = n_heads // hp

    def page_spec(slot):
        return pl.BlockSpec((None, None, page, n_heads, HEAD_W),
                            lambda b, j, pt: (layer, pt[b, j * n_slots + slot], 0, 0, 0))

    row_spec = pl.BlockSpec((dec_seq, d_model), lambda b, j, pt: (b, 0))
    new_spec = pl.BlockSpec((None, dec_seq, n_heads, HEAD_W), lambda b, j, pt: (b, 0, 0, 0))
    kern = functools.partial(_attn_sample_kernel, n_heads=n_heads, n_slots=n_slots, dec_seq=dec_seq,
                             lam_init=lam_init)
    grid_spec = pltpu.PrefetchScalarGridSpec(
        num_scalar_prefetch=1,
        grid=(n_dec, n_pages // n_slots),
        in_specs=([row_spec] + [page_spec(s) for s in range(n_slots)] * 2 + [new_spec, new_spec]
                  + [pl.BlockSpec((4, HEAD_DIM), lambda b, j, pt: (0, 0)),
                     pl.BlockSpec((HEAD_W, 1), lambda b, j, pt: (0, 0))]),
        out_specs=row_spec,
        scratch_shapes=[
            pltpu.VMEM((n_groups, width, HEAD_W), BF16),
            pltpu.VMEM((n_slots, n_groups, hp, width), F32),
            pltpu.VMEM((n_slots, n_groups, hp, width), F32),
            pltpu.VMEM((n_slots, n_groups, HEAD_W, width), F32),
        ],
    )
    new_rows = lambda a: a.reshape(n_dec, dec_seq, n_heads, HEAD_W)
    return pl.pallas_call(
        kern,
        grid_spec=grid_spec,
        out_shape=jax.ShapeDtypeStruct((m_rows, d_model), F32),
        compiler_params=_params("arbitrary", "arbitrary"),
    )(page_table, q, *([cache_k] * n_slots), *([cache_v] * n_slots), new_rows(kn), new_rows(vn), lamv, subln_col)


def _tile(n, target):
    t = min(n, target)
    assert n % t == 0
    return t


def kernel(x_prompt, x_sample, cache_k, cache_v, page_table, gm_w_in, gm_b_in, gm_ln_g, gm_ln_b, gm_w_s, gm_b_s,
           gm_w_out, gm_b_out, at_w_qkv, at_lambda_q1, at_lambda_k1, at_lambda_q2, at_lambda_k2, at_subln_g,
           at_w_out, ln_mix_g, ln_mix_b, ln_ffn_g, ln_ffn_b, ffn_w_in, ffn_w_out):
    n_batch, seq, d_model = x_prompt.shape
    n_dec, dec_seq, _ = x_sample.shape
    depth = ln_mix_g.shape[0]
    n_heads = d_model // HEAD_W
    n_pages = page_table.shape[1]
    past_len = n_pages * cache_k.shape[2]
    alpha = (2 * depth) ** 0.25
    inner = gm_w_in.shape[2] // 2
    d_ff = ffn_w_out.shape[1]

    xp = x_prompt.reshape(n_batch * seq, d_model)
    xs = x_sample.reshape(n_dec * dec_seq, d_model)
    mp, ms = xp.shape[0], xs.shape[0]

    tm_p = _tile(seq, 1024)
    tm_r = _tile(seq, 512)
    tm_g = _tile(seq, 512)
    tn = _tile(inner, 512)
    tf = 256 if d_ff % 256 == 0 else d_ff

    tables_p = _rope_tables(jnp.arange(seq, dtype=jnp.int32))
    tables_s = _rope_tables(jnp.tile(past_len + jnp.arange(dec_seq, dtype=jnp.int32), n_dec))

    gm_v_p, gm_v_s, k_p, v_p, k_s, v_s = [], [], [], [], [], []
    for i in range(depth):
        j = i // 2
        if i % 2 == 0:
            bexp = jnp.repeat(gm_b_s[j].T, GROUP_W, axis=1)
            wexp = jnp.repeat(jnp.transpose(gm_w_s[j][:, :dec_seq, :dec_seq], (2, 1, 0)), GROUP_W, axis=2)
            gated_p, gv_p = _gmlp_in_prompt(xp, gm_w_in, gm_b_in, gm_ln_g, gm_ln_b, gm_w_s, bexp, j, seq,
                                            tm=tm_g, tn=tn)
            gated_s, gv_s = _gmlp_in_sample(xs, gm_w_in, gm_b_in, gm_ln_g, gm_ln_b, wexp, bexp[:dec_seq], j,
                                            dec_seq, tn=tn)
            gm_v_p.append(gv_p)
            gm_v_s.append(gv_s.reshape(n_dec, dec_seq, inner))
            mix_p = dict(a=gated_p, w=gm_w_out, bias=gm_b_out)
            mix_s = dict(a=gated_s, w=gm_w_out, bias=gm_b_out)
        else:
            lam_init = 0.8 - 0.6 * math.exp(-0.3 * i)
            lamv = jnp.stack([at_lambda_q1[j], at_lambda_k1[j], at_lambda_q2[j], at_lambda_k2[j]])
            subln = at_subln_g[j].reshape(HEAD_W, 1)
            scale = HEAD_DIM ** -0.5 * math.log2(math.e)
            tq = _tile(seq, 512)
            proj = functools.partial(_proj_rope, xp, xs, at_w_qkv, j, n_cols=d_model, tm=tm_r)
            qb, qs = proj(col0=0, tables=tables_p, tables_s=tables_s, pos_blocks=seq // tm_r,
                          out_dtypes=(BF16,), scale=scale)
            kf, kb, kn = proj(col0=d_model, tables=tables_p, tables_s=tables_s, pos_blocks=seq // tm_r,
                              out_dtypes=(F32, BF16), scale=1.0)
            vf, vn, vt = proj(col0=2 * d_model, tables=None, tables_s=None, pos_blocks=1,
                              out_dtypes=(F32,), scale=1.0, vt=(seq, tq))
            a_p = _attn_prompt(qb, kb, vt, lamv, subln, n_batch, seq, lam_init=lam_init, tq=tq)
            a_s = _attn_sample(qs, cache_k, cache_v, page_table, kn, vn, lamv, subln, j, dec_seq,
                               lam_init=lam_init, n_slots=math.gcd(n_pages, 8))
            k_p.append(kf.reshape(n_batch, seq, n_heads, HEAD_W))
            v_p.append(vf.reshape(n_batch, seq, n_heads, HEAD_W))
            k_s.append(kn.reshape(n_dec, dec_seq, n_heads, HEAD_W))
            v_s.append(vn.reshape(n_dec, dec_seq, n_heads, HEAD_W))
            mix_p = dict(a=a_p, w=at_w_out, bias=None)
            mix_s = dict(a=a_s, w=at_w_out, bias=None)
        xp, xs = _proj_ln(mix_p["a"], mix_s["a"], mix_p["w"], j, mix_p["bias"], xp, xs, ln_mix_g, ln_mix_b, i,
                          alpha=alpha, tm=tm_r)
        xp, xs = _ffn(xp, xs, ffn_w_in, ffn_w_out, i, ln_ffn_g, ln_ffn_b, alpha=alpha, tm=tm_p, tf=tf)

    return (xp.reshape(n_batch, seq, d_model), xs.reshape(n_dec, dec_seq, d_model),
            jnp.stack(gm_v_p), jnp.stack(gm_v_s), jnp.stack(k_p), jnp.stack(v_p), jnp.stack(k_s), jnp.stack(v_s))
```

```python
import functools
import math

import jax
import jax.numpy as jnp
from jax import lax
from jax.experimental import pallas as pl
from jax.experimental.pallas import tpu as pltpu

F32 = jnp.float32
BF16 = jnp.bfloat16

LN_EPS = 1e-5
CHUNK = 128
GROUP_W = 128
HEAD_DIM = 64
HEAD_W = 2 * HEAD_DIM
ROT_DIM = HEAD_DIM // 4
ROPE_THETA = 500000.0
LANES = 128
SUBLANES = 8
MXU_COLS = 256
DENOM_ROWS = 16
LN_ROWS = 128
SOFTMAX_ROWS = 64
VMEM_LIMIT_BYTES = 60 * 1024 * 1024


def _params(*sem):
    return pltpu.CompilerParams(dimension_semantics=sem, vmem_limit_bytes=VMEM_LIMIT_BYTES)


def _layer_norm(y, g, b):
    mu = jnp.mean(y, axis=-1, keepdims=True)
    yc = y - mu
    var = jnp.mean(yc * yc, axis=-1, keepdims=True)
    return yc * lax.rsqrt(var + LN_EPS) * g + b


def _residual_ln_inplace(o_ref, x_ref, bias_ref, g_ref, b_ref, alpha, tm):
    block = min(tm, LN_ROWS)
    assert tm % block == 0

    def body(r, carry):
        rows = pl.ds(pl.multiple_of(r * block, block), block)
        y = alpha * x_ref[rows, :] + o_ref[rows, :]
        if bias_ref is not None:
            y = y + bias_ref[...]
        o_ref[rows, :] = _layer_norm(y, g_ref[...], b_ref[...])
        return carry

    lax.fori_loop(0, tm // block, body, 0)


def _cast_rows_to_bf16(dst_ref, src_ref):
    rows = 256
    n = src_ref.shape[0]
    assert n % rows == 0

    def body(r, carry):
        sl = pl.ds(pl.multiple_of(r * rows, rows), rows)
        dst_ref[sl, :] = src_ref[sl, :].astype(BF16)
        return carry

    lax.fori_loop(0, n // rows, body, 0)


def _gmlp_project(x_ref, w_ref, b_ref, xb_s, z_s, wb_s=None):
    j = pl.program_id(1)

    @pl.when(j == 0)
    def _():
        xb_s[...] = x_ref[...].astype(BF16)

    if wb_s is None:
        wb = w_ref[...].astype(BF16)
    else:
        @pl.when(pl.program_id(0) == 0)
        def _():
            wb_s[j] = w_ref[...].astype(BF16)

        wb = wb_s[j]
    z = jnp.dot(xb_s[...], wb, preferred_element_type=F32) + b_ref[...]
    z_s[j] = 0.5 * z * (1.0 + lax.erf(z * math.sqrt(0.5)))


def _gmlp_v_layer_norm(z_s, lng_ref, lnb_ref, rows, n_half, tn, inner):
    vs = [z_s[n_half + c, rows, :] for c in range(n_half)]
    mu = sum(jnp.sum(v, axis=-1, keepdims=True) for v in vs) / inner
    var = sum(jnp.sum(jnp.square(v - mu), axis=-1, keepdims=True) for v in vs) / inner
    rstd = lax.rsqrt(var + LN_EPS)
    return [(vs[c] - mu) * rstd * lng_ref[:, c * tn:(c + 1) * tn] + lnb_ref[:, c * tn:(c + 1) * tn]
            for c in range(n_half)]


def _gmlp_in_prompt_kernel(x_ref, w_ref, b_ref, lng_ref, lnb_ref, ws_ref, bexp_ref,
                           gated_ref, gmv_ref, xb_s, z_s, vn_s, wb_s, *, tm, tn, inner):
    _gmlp_project(x_ref, w_ref, b_ref, xb_s, z_s, wb_s)
    n_half = inner // tn
    n_groups = inner // GROUP_W
    n_rc = tm // CHUNK
    per_chunk = tn // GROUP_W

    @pl.when(pl.program_id(1) == pl.num_programs(1) - 1)
    def _():
        for r in range(n_rc):
            rows = slice(r * CHUNK, (r + 1) * CHUNK)
            vn = _gmlp_v_layer_norm(z_s, lng_ref, lnb_ref, rows, n_half, tn, inner)
            for c in range(n_half):
                if r == n_rc - 1:
                    gmv_ref[:, c * tn:(c + 1) * tn] = vn[c]
                for q in range(per_chunk):
                    g = c * per_chunk + q
                    vn_s[g, :, r * CHUNK:(r + 1) * CHUNK] = vn[c][:, q * GROUP_W:(q + 1) * GROUP_W].astype(BF16)
        t_idx = lax.broadcasted_iota(jnp.int32, (CHUNK, CHUNK), 0)
        s_idx = lax.broadcasted_iota(jnp.int32, (CHUNK, CHUNK), 1)
        causal = s_idx <= t_idx
        for g in range(n_groups):
            w_causal = jnp.where(causal, ws_ref[g], 0.0).astype(BF16)
            mixed = jnp.dot(w_causal, vn_s[g], preferred_element_type=F32)
            cols = slice(g * GROUP_W, (g + 1) * GROUP_W)
            c, q = divmod(g, per_chunk)
            for r in range(n_rc):
                rows = slice(r * CHUNK, (r + 1) * CHUNK)
                u = z_s[c, rows, q * GROUP_W:(q + 1) * GROUP_W]
                m = mixed[:, r * CHUNK:(r + 1) * CHUNK] + bexp_ref[:, cols]
                gated_ref[rows, cols] = (u * m).astype(BF16)


def _gmlp_in_prompt(x, w_in, b_in, ln_g, ln_b, w_s, bexp, layer, seq, *, tm, tn):
    m_rows, d_model = x.shape
    inner = w_in.shape[2] // 2
    n_groups = inner // GROUP_W
    n_batch = m_rows // seq
    assert seq % tm == 0 and tm % CHUNK == 0 and inner % tn == 0 and tn % GROUP_W == 0
    tiles_per_seq = seq // tm
    nj = 2 * inner // tn
    kern = functools.partial(_gmlp_in_prompt_kernel, tm=tm, tn=tn, inner=inner)
    return pl.pallas_call(
        kern,
        grid=(m_rows // tm, nj),
        in_specs=[
            pl.BlockSpec((tm, d_model), lambda i, j: (i, 0)),
            pl.BlockSpec((None, d_model, tn), lambda i, j: (layer, 0, jnp.where(i == 0, j, nj - 1))),
            pl.BlockSpec((None, 1, tn), lambda i, j: (layer, 0, j)),
            pl.BlockSpec((None, 1, inner), lambda i, j: (layer, 0, 0)),
            pl.BlockSpec((None, 1, inner), lambda i, j: (layer, 0, 0)),
            pl.BlockSpec((None, n_groups, CHUNK, CHUNK), lambda i, j: (layer, 0, 0, 0)),
            pl.BlockSpec((CHUNK, inner), lambda i, j: (0, 0)),
        ],
        out_specs=[
            pl.BlockSpec((tm, inner), lambda i, j: (i, 0)),
            pl.BlockSpec((None, CHUNK, inner), lambda i, j: (i // tiles_per_seq, 0, 0)),
        ],
        out_shape=[
            jax.ShapeDtypeStruct((m_rows, inner), BF16),
            jax.ShapeDtypeStruct((n_batch, CHUNK, inner), F32),
        ],
        scratch_shapes=[
            pltpu.VMEM((tm, d_model), BF16),
            pltpu.VMEM((nj, tm, tn), F32),
            pltpu.VMEM((n_groups, CHUNK, tm), BF16),
            pltpu.VMEM((nj, d_model, tn), BF16),
        ],
        compiler_params=_params("arbitrary", "arbitrary"),
    )(x, w_in, b_in.reshape(b_in.shape[0], 1, -1), ln_g.reshape(ln_g.shape[0], 1, -1),
      ln_b.reshape(ln_b.shape[0], 1, -1), w_s, bexp)


def _gmlp_in_sample_kernel(x_ref, w_ref, b_ref, lng_ref, lnb_ref, wexp_ref, bexp_ref,
                           gated_ref, gmv_ref, xb_s, z_s, *, tm, tn, inner, dec_seq):
    _gmlp_project(x_ref, w_ref, b_ref, xb_s, z_s)
    n_half = inner // tn

    @pl.when(pl.program_id(1) == pl.num_programs(1) - 1)
    def _():
        vn = _gmlp_v_layer_norm(z_s, lng_ref, lnb_ref, slice(0, tm), n_half, tn, inner)
        t_idx = lax.broadcasted_iota(jnp.int32, (dec_seq, tn), 0)
        for c in range(n_half):
            cols = slice(c * tn, (c + 1) * tn)
            gmv_ref[:, cols] = vn[c]
            for b in range(tm // dec_seq):
                rows = slice(b * dec_seq, (b + 1) * dec_seq)
                vb = vn[c][rows, :]
                mixed = bexp_ref[:, cols]
                for s in range(dec_seq):
                    w_ts = jnp.where(t_idx >= s, wexp_ref[s, :, cols], 0.0)
                    mixed = mixed + w_ts * vb[s:s + 1, :]
                gated_ref[rows, cols] = (z_s[c, rows, :] * mixed).astype(BF16)


def _gmlp_in_sample(x, w_in, b_in, ln_g, ln_b, wexp, bexp, layer, dec_seq, *, tn):
    m_rows, d_model = x.shape
    inner = w_in.shape[2] // 2
    assert dec_seq % 8 == 0 and dec_seq <= CHUNK and inner % tn == 0
    nj = 2 * inner // tn
    kern = functools.partial(_gmlp_in_sample_kernel, tm=m_rows, tn=tn, inner=inner, dec_seq=dec_seq)
    return pl.pallas_call(
        kern,
        grid=(1, nj),
        in_specs=[
            pl.BlockSpec((m_rows, d_model), lambda i, j: (0, 0)),
            pl.BlockSpec((None, d_model, tn), lambda i, j: (layer, 0, j)),
            pl.BlockSpec((None, 1, tn), lambda i, j: (layer, 0, j)),
            pl.BlockSpec((None, 1, inner), lambda i, j: (layer, 0, 0)),
            pl.BlockSpec((None, 1, inner), lambda i, j: (layer, 0, 0)),
            pl.BlockSpec((dec_seq, dec_seq, inner), lambda i, j: (0, 0, 0)),
            pl.BlockSpec((dec_seq, inner), lambda i, j: (0, 0)),
        ],
        out_specs=[
            pl.BlockSpec((m_rows, inner), lambda i, j: (0, 0)),
            pl.BlockSpec((m_rows, inner), lambda i, j: (0, 0)),
        ],
        out_shape=[
            jax.ShapeDtypeStruct((m_rows, inner), BF16),
            jax.ShapeDtypeStruct((m_rows, inner), F32),
        ],
        scratch_shapes=[
            pltpu.VMEM((m_rows, d_model), BF16),
            pltpu.VMEM((nj, m_rows, tn), F32),
        ],
        compiler_params=_params("arbitrary", "arbitrary"),
    )(x, w_in, b_in.reshape(b_in.shape[0], 1, -1), ln_g.reshape(ln_g.shape[0], 1, -1),
      ln_b.reshape(ln_b.shape[0], 1, -1), wexp, bexp)


def _proj_ln_kernel(*refs, alpha, has_bias):
    if has_bias:
        a_ref, as_ref, w_ref, bias_ref, x_ref, xs_ref, g_ref, b_ref, o_ref, os_ref, wb_s = refs
    else:
        a_ref, as_ref, w_ref, x_ref, xs_ref, g_ref, b_ref, o_ref, os_ref, wb_s = refs
        bias_ref = None

    def apply(a, x_in, o):
        o[...] = jnp.dot(a[...].astype(BF16), wb_s[...], preferred_element_type=F32)
        _residual_ln_inplace(o, x_in, bias_ref, g_ref, b_ref, alpha, o.shape[0])

    @pl.when(pl.program_id(0) == 0)
    def _():
        _cast_rows_to_bf16(wb_s, w_ref)
        apply(as_ref, xs_ref, os_ref)

    apply(a_ref, x_ref, o_ref)


def _proj_ln(a, a_s, w, layer, bias, x, xs, ln_g, ln_b, ln_idx, *, alpha, tm):
    m_rows, k_dim = a.shape
    ms = a_s.shape[0]
    d_model = x.shape[1]
    assert m_rows % tm == 0
    vec = lambda idx: pl.BlockSpec((None, 1, d_model), lambda i: (idx, 0, 0))
    in_specs = [pl.BlockSpec((tm, k_dim), lambda i: (i, 0)),
                pl.BlockSpec((ms, k_dim), lambda i: (0, 0)),
                pl.BlockSpec((None, k_dim, d_model), lambda i: (layer, 0, 0), pipeline_mode=pl.Buffered(1))]
    args = [a, a_s, w]
    if bias is not None:
        in_specs.append(vec(layer))
        args.append(bias.reshape(bias.shape[0], 1, -1))
    in_specs += [pl.BlockSpec((tm, d_model), lambda i: (i, 0)), pl.BlockSpec((ms, d_model), lambda i: (0, 0)),
                 vec(ln_idx), vec(ln_idx)]
    args += [x, xs, ln_g.reshape(ln_g.shape[0], 1, -1), ln_b.reshape(ln_b.shape[0], 1, -1)]
    kern = functools.partial(_proj_ln_kernel, alpha=alpha, has_bias=bias is not None)
    return pl.pallas_call(
        kern,
        grid=(m_rows // tm,),
        in_specs=in_specs,
        out_specs=[pl.BlockSpec((tm, d_model), lambda i: (i, 0)), pl.BlockSpec((ms, d_model), lambda i: (0, 0))],
        out_shape=[jax.ShapeDtypeStruct((m_rows, d_model), F32), jax.ShapeDtypeStruct((ms, d_model), F32)],
        scratch_shapes=[pltpu.VMEM((k_dim, d_model), BF16)],
        compiler_params=_params("arbitrary"),
    )(*args)


def _ffn_kernel(x_ref, xs_ref, wg_ref, wu_ref, wo_ref, g_ref, b_ref, o_ref, os_ref, xb_s, xsb_s, *, alpha):
    i, f = pl.program_id(0), pl.program_id(1)
    last = pl.num_programs(1) - 1

    def start(x_in, xb, o):
        @pl.when(f == 0)
        def _():
            xb[...] = x_in[...].astype(BF16)
            o[...] = jnp.zeros_like(o)

    def accumulate(xb, o, wg, wu, wo):
        gate = jnp.dot(xb[...], wg, preferred_element_type=F32)
        up = jnp.dot(xb[...], wu, preferred_element_type=F32)
        h = (jax.nn.silu(gate) * up).astype(BF16)
        o[...] += jnp.dot(h, wo, preferred_element_type=F32)

    def finish(x_in, o):
        @pl.when(f == last)
        def _():
            _residual_ln_inplace(o, x_in, None, g_ref, b_ref, alpha, o.shape[0])

    start(x_ref, xb_s, o_ref)
    wg = wg_ref[...].astype(BF16)
    wu = wu_ref[...].astype(BF16)
    wo = wo_ref[...].astype(BF16)
    accumulate(xb_s, o_ref, wg, wu, wo)
    finish(x_ref, o_ref)

    @pl.when(i == 0)
    def _():
        start(xs_ref, xsb_s, os_ref)
        accumulate(xsb_s, os_ref, wg, wu, wo)
        finish(xs_ref, os_ref)


def _ffn(x, xs, w_in, w_out, layer, ln_g, ln_b, *, alpha, tm, tf):
    m_rows, d_model = x.shape
    ms = xs.shape[0]
    d_ff = w_out.shape[1]
    assert m_rows % tm == 0 and d_ff % tf == 0
    nf = d_ff // tf
    vec = pl.BlockSpec((None, 1, d_model), lambda i, f: (layer, 0, 0))
    kern = functools.partial(_ffn_kernel, alpha=alpha)
    return pl.pallas_call(
        kern,
        grid=(m_rows // tm, nf),
        in_specs=[
            pl.BlockSpec((tm, d_model), lambda i, f: (i, 0)),
            pl.BlockSpec((ms, d_model), lambda i, f: (0, 0)),
            pl.BlockSpec((None, d_model, tf), lambda i, f: (layer, 0, f)),
            pl.BlockSpec((None, d_model, tf), lambda i, f: (layer, 0, nf + f)),
            pl.BlockSpec((None, tf, d_model), lambda i, f: (layer, f, 0)),
            vec, vec,
        ],
        out_specs=[pl.BlockSpec((tm, d_model), lambda i, f: (i, 0)),
                   pl.BlockSpec((ms, d_model), lambda i, f: (0, 0))],
        out_shape=[jax.ShapeDtypeStruct((m_rows, d_model), F32), jax.ShapeDtypeStruct((ms, d_model), F32)],
        scratch_shapes=[pltpu.VMEM((tm, d_model), BF16), pltpu.VMEM((ms, d_model), BF16)],
        compiler_params=_params("arbitrary", "arbitrary"),
    )(x, xs, w_in, w_in, w_out, ln_g.reshape(ln_g.shape[0], 1, -1), ln_b.reshape(ln_b.shape[0], 1, -1))


def _rope_tables(pos):
    half = ROT_DIM // 2
    inv = jnp.power(ROPE_THETA, -jnp.arange(half, dtype=F32) * 2.0 / ROT_DIM)
    ang = pos.astype(F32)[:, None] * inv[None, :]
    cos, sin = jnp.cos(ang), jnp.sin(ang)
    n = pos.shape[0]
    rest = HEAD_DIM - ROT_DIM
    c = jnp.concatenate([cos, cos, jnp.ones((n, rest), F32)], axis=1)
    s_next = jnp.concatenate([-sin, jnp.zeros((n, half + rest), F32)], axis=1)
    s_prev = jnp.concatenate([jnp.zeros((n, half), F32), sin, jnp.zeros((n, rest), F32)], axis=1)
    return tuple(jnp.tile(t, (1, HEAD_W // HEAD_DIM)) for t in (c, s_next, s_prev))


def _proj_rope_kernel(*refs, rope, scale, n_out, vt_tk):
    n_tab = 3 if rope else 0
    x_ref, xs_ref, w_ref = refs[:3]
    tabs, tabs_s = refs[3:3 + n_tab], refs[3 + n_tab:3 + 2 * n_tab]
    k = 3 + 2 * n_tab
    out_refs, os_ref = refs[k:k + n_out], refs[k + n_out]
    vt_ref = refs[k + n_out + 1] if vt_tk else None
    wb_s = refs[-1]

    def emit(x_in, tables, outs, vt):
        y = jnp.dot(x_in[...].astype(BF16), wb_s[...], preferred_element_type=F32)
        for h in range(y.shape[1] // HEAD_W):
            cols = slice(h * HEAD_W, (h + 1) * HEAD_W)
            yh = y[:, cols]
            if rope:
                c_ref, sn_ref, sp_ref = tables
                half = ROT_DIM // 2
                yh = (yh * c_ref[...] + pltpu.roll(yh, HEAD_W - half, 1) * sn_ref[...]
                      + pltpu.roll(yh, half, 1) * sp_ref[...])
            if scale != 1.0:
                yh = yh * scale
            for o_ref in outs:
                o_ref[:, cols] = yh.astype(o_ref.dtype)
            if vt is not None:
                for kb in range(y.shape[0] // vt_tk):
                    vt[h, kb] = yh[kb * vt_tk:(kb + 1) * vt_tk, :].T.astype(BF16)

    @pl.when(pl.program_id(0) == 0)
    def _():
        _cast_rows_to_bf16(wb_s, w_ref)
        emit(xs_ref, tabs_s, (os_ref,), None)

    emit(x_ref, tabs, out_refs, vt_ref)


def _proj_rope(x, xs, w, layer, col0, n_cols, tables, tables_s, pos_blocks, out_dtypes, *, scale, tm, vt=None):
    m_rows, d_model = x.shape
    ms = xs.shape[0]
    assert m_rows % tm == 0 and col0 % n_cols == 0 and n_cols % HEAD_W == 0
    rope = tables is not None
    in_specs = [pl.BlockSpec((tm, d_model), lambda i: (i, 0)),
                pl.BlockSpec((ms, d_model), lambda i: (0, 0)),
                pl.BlockSpec((None, d_model, n_cols), lambda i: (layer, 0, col0 // n_cols),
                             pipeline_mode=pl.Buffered(1))]
    args = [x, xs, w]
    if rope:
        in_specs += [pl.BlockSpec((tm, HEAD_W), lambda i: (i % pos_blocks, 0))] * 3
        in_specs += [pl.BlockSpec((ms, HEAD_W), lambda i: (0, 0))] * 3
        args += list(tables) + list(tables_s)
    out_specs = [pl.BlockSpec((tm, n_cols), lambda i: (i, 0)) for _ in out_dtypes]
    out_shape = [jax.ShapeDtypeStruct((m_rows, n_cols), dt) for dt in out_dtypes]
    out_specs.append(pl.BlockSpec((ms, n_cols), lambda i: (0, 0)))
    out_shape.append(jax.ShapeDtypeStruct((ms, n_cols), F32))
    vt_tk = 0
    if vt is not None:
        seq, vt_tk = vt
        assert seq % tm == 0 and tm % vt_tk == 0
        tiles_per_seq, n_heads = seq // tm, n_cols // HEAD_W
        out_specs.append(pl.BlockSpec((n_heads, tm // vt_tk, HEAD_W, vt_tk),
                                      lambda i: (i // tiles_per_seq, i % tiles_per_seq, 0, 0)))
        out_shape.append(jax.ShapeDtypeStruct(((m_rows // seq) * n_heads, seq // vt_tk, HEAD_W, vt_tk), BF16))
    kern = functools.partial(_proj_rope_kernel, rope=rope, scale=scale, n_out=len(out_dtypes), vt_tk=vt_tk)
    return pl.pallas_call(
        kern,
        grid=(m_rows // tm,),
        in_specs=in_specs,
        out_specs=out_specs,
        out_shape=out_shape,
        scratch_shapes=[pltpu.VMEM((d_model, n_cols), BF16)],
        compiler_params=_params("arbitrary"),
    )(*args)


def _diff_lambda(lam_ref, lam_init):
    lv = lam_ref[...]
    e1 = jnp.exp(jnp.sum(lv[0:1, :] * lv[1:2, :], axis=-1, keepdims=True))
    e2 = jnp.exp(jnp.sum(lv[2:3, :] * lv[3:4, :], axis=-1, keepdims=True))
    return e1 - e2 + lam_init


def _map_masks(rows):
    lane = lax.broadcasted_iota(jnp.int32, (rows, HEAD_W), 1)
    return lane < HEAD_DIM, lane >= HEAD_DIM


def _sub_ln(o, g_col, lam_init):
    return o * lax.rsqrt(jnp.mean(o * o, axis=0, keepdims=True) + LN_EPS) * g_col * (1.0 - lam_init)


def _prompt_attention(q_ref, k_ref, vt_ref, lam_ref, g_ref, o_ref, q2_s, m_s, acc_s, c_s, mx_s, s_s, p_s,
                      *, tq, lam_init, q_steps):
    gw = min(MXU_COLS, tq)
    n_g = 2 * tq // gw
    m0, m1 = _map_masks(tq)
    lam = _diff_lambda(lam_ref, lam_init)

    def block(ki, diagonal, buf):
        kb = k_ref[ki * tq:(ki + 1) * tq, :]
        vtb = jnp.concatenate([vt_ref[ki], jnp.ones((DENOM_ROWS, tq), BF16)], axis=0)
        n_keys = [min(tq, (g * gw) % tq + gw) if diagonal else tq for g in range(n_g)]
        for g in range(n_g):
            cols, nk = slice(g * gw, (g + 1) * gw), n_keys[g]
            s = lax.dot_general(kb[:nk], q2_s[cols, :], (((1,), (1,)), ((), ())), preferred_element_type=F32)
            if diagonal:
                key = lax.broadcasted_iota(jnp.int32, (nk, gw), 0)
                qry = lax.broadcasted_iota(jnp.int32, (nk, gw), 1) + (g * gw) % tq
                s = jnp.where(key <= qry, s, -jnp.inf)
            s_s[buf, g, 0:nk, :] = s
            mx_s[:, cols] = jnp.max(s, axis=0, keepdims=True)
        for g in range(n_g):
            cols, nk = slice(g * gw, (g + 1) * gw), n_keys[g]
            m_old = m_s[:, cols]
            m_new = jnp.maximum(m_old, mx_s[:, cols])
            c_s[:, cols] = jnp.exp2(m_old - m_new)
            for r in range(0, nk, SOFTMAX_ROWS):
                p_s[buf, g, r:r + SOFTMAX_ROWS, :] = jnp.exp2(s_s[buf, g, r:r + SOFTMAX_ROWS, :] - m_new).astype(BF16)
            m_s[:, cols] = m_new
        for g in range(n_g):
            cols, nk = slice(g * gw, (g + 1) * gw), n_keys[g]
            acc_s[:, cols] = c_s[:, cols] * acc_s[:, cols] + jnp.dot(vtb[:, :nk], p_s[buf, g, 0:nk, :],
                                                                     preferred_element_type=F32)

    n_blocks = 0
    for qi in q_steps:
        rows = slice(qi * tq, (qi + 1) * tq)
        q = q_ref[rows, :]
        zero = jnp.zeros_like(q)
        q2_s[0:tq, :] = jnp.where(m0, q, zero)
        q2_s[tq:2 * tq, :] = jnp.where(m1, q, zero)
        m_s[...] = jnp.full_like(m_s, -jnp.inf)
        acc_s[...] = jnp.zeros_like(acc_s)
        for ki in range(qi + 1):
            block(ki, ki == qi, n_blocks % 2)
            n_blocks += 1
        o = acc_s[0:HEAD_W, :] / acc_s[HEAD_W:HEAD_W + 1, :]
        o = o[:, :tq] - lam * o[:, tq:]
        o_ref[rows, :] = _sub_ln(o, g_ref[...], lam_init).T.astype(o_ref.dtype)


def _sample_attention(j, n_steps, q_ref, k_pages, v_pages, kn_ref, vn_ref, lam_ref, g_ref, o_ref,
                      qm_s, m_s, l_s, acc_s, *, n_heads, dec_seq, lam_init):
    n_slots = len(k_pages)
    hp = min(n_heads, SUBLANES)
    n_groups = n_heads // hp
    cw = 2 * dec_seq
    width = hp * cw

    @pl.when(j == 0)
    def _():
        m0, m1 = _map_masks(dec_seq)
        for h in range(n_heads):
            grp, hl = divmod(h, hp)
            qh = q_ref[:, h * HEAD_W:(h + 1) * HEAD_W]
            qm_s[grp, hl * cw:(hl + 1) * cw, :] = jnp.concatenate(
                [jnp.where(m0, qh, 0.0), jnp.where(m1, qh, 0.0)], axis=0).astype(BF16)
        m_s[...] = jnp.full_like(m_s, -jnp.inf)
        l_s[...] = jnp.zeros_like(l_s)
        acc_s[...] = jnp.zeros_like(acc_s)

    own = (lax.broadcasted_iota(jnp.int32, (hp, width), 1) // cw
           == lax.broadcasted_iota(jnp.int32, (hp, width), 0))

    def update(slot, k3_ref, v3_ref, causal):
        n_keys = k3_ref.shape[0]
        rows = n_keys * hp
        for grp in range(n_groups):
            heads = slice(grp * hp, (grp + 1) * hp)
            kr = k3_ref[:, heads, :].reshape(rows, HEAD_W).astype(BF16)
            u = lax.dot_general(kr, qm_s[grp], (((1,), (1,)), ((), ())), preferred_element_type=F32)
            u = u.reshape(n_keys, hp, width)
            if causal:
                key = lax.broadcasted_iota(jnp.int32, (n_keys, hp, width), 0)
                qry = lax.broadcasted_iota(jnp.int32, (n_keys, hp, width), 2) % dec_seq
                u = jnp.where(key <= qry, u, -jnp.inf)
            m_old = m_s[slot, grp]
            m_new = jnp.maximum(m_old, jnp.max(u, axis=0))
            p = jnp.where(own, jnp.exp2(u - m_new), 0.0)
            corr = jnp.exp2(m_old - m_new)
            l_s[slot, grp] = corr * l_s[slot, grp] + jnp.sum(p, axis=0)
            m_s[slot, grp] = m_new
            vr = v3_ref[:, heads, :].reshape(rows, HEAD_W).astype(BF16)
            pv = lax.dot_general(vr, p.reshape(rows, width).astype(BF16), (((0,), (0,)), ((), ())),
                                 preferred_element_type=F32)
            corr_col = jnp.sum(jnp.where(own, corr, 0.0), axis=0, keepdims=True)
            acc_s[slot, grp] = acc_s[slot, grp] * corr_col + pv

    for slot in range(n_slots):
        update(slot, k_pages[slot], v_pages[slot], False)

    @pl.when(j == n_steps - 1)
    def _():
        update(0, kn_ref, vn_ref, True)
        lam = _diff_lambda(lam_ref, lam_init)
        for grp in range(n_groups):
            m_all = m_s[0, grp]
            for slot in range(1, n_slots):
                m_all = jnp.maximum(m_all, m_s[slot, grp])
            l_all = jnp.zeros((hp, width), F32)
            acc = jnp.zeros((HEAD_W, width), F32)
            for slot in range(n_slots):
                w = jnp.exp2(m_s[slot, grp] - m_all)
                l_all = l_all + w * l_s[slot, grp]
                acc = acc + acc_s[slot, grp] * jnp.sum(jnp.where(own, w, 0.0), axis=0, keepdims=True)
            o = acc / jnp.sum(jnp.where(own, l_all, 0.0), axis=0, keepdims=True)
            o = o - lam * pltpu.roll(o, width - dec_seq, 1)
            ot = _sub_ln(o, g_ref[...], lam_init).T
            for hl in range(hp):
                h = grp * hp + hl
                o_ref[:, h * HEAD_W:(h + 1) * HEAD_W] = ot[hl * cw:hl * cw + dec_seq, :]


def _attn_kernel(pt_ref, qp_ref, kp_ref, vtp_ref, qs_ref, *refs, n_slots, n_heads, dec_seq, tq, lam_init,
                 q_split, prompt_steps, sample_steps, steps_per_seq):
    del pt_ref
    k_pages, v_pages = refs[:n_slots], refs[n_slots:2 * n_slots]
    kn_ref, vn_ref, lam_ref, g_ref, op_ref, os_ref = refs[2 * n_slots:2 * n_slots + 6]
    prompt_scratch, sample_scratch = refs[2 * n_slots + 6:-4], refs[-4:]
    t = pl.program_id(0)

    for part, q_steps in enumerate(q_split):
        @pl.when(jnp.logical_and(t < prompt_steps, t % len(q_split) == part))
        def _():
            _prompt_attention(qp_ref, kp_ref, vtp_ref, lam_ref, g_ref, op_ref, *prompt_scratch,
                              tq=tq, lam_init=lam_init, q_steps=q_steps)

    @pl.when(t < sample_steps)
    def _():
        _sample_attention(t % steps_per_seq, steps_per_seq, qs_ref, k_pages, v_pages, kn_ref, vn_ref, lam_ref, g_ref,
                          os_ref, *sample_scratch, n_heads=n_heads, dec_seq=dec_seq, lam_init=lam_init)


def _attention(qb, kb, vt, qs, cache_k, cache_v, page_table, kn, vn, lamv, subln_col, layer, n_batch, seq, dec_seq,
               *, lam_init, tq, n_slots):
    m_rows, d_model = qb.shape
    ms = qs.shape[0]
    n_heads = d_model // HEAD_W
    n_dec, n_pages = page_table.shape
    page = cache_k.shape[2]
    hp = min(n_heads, SUBLANES)
    nq = seq // tq
    gw = min(MXU_COLS, tq)
    assert seq % tq == 0 and tq % SOFTMAX_ROWS == 0 and vt.shape[1:] == (nq, HEAD_W, tq)
    assert n_pages % n_slots == 0 and dec_seq % SUBLANES == 0 and n_heads % hp == 0
    width = hp * 2 * dec_seq
    n_groups = n_heads // hp
    q_split = [tuple(range(nq - 1)), (nq - 1,)] if nq > 1 else [(0,)]
    n_units, parts = n_batch * n_heads, len(q_split)
    steps_per_seq = n_pages // n_slots
    prompt_steps, sample_steps = n_units * parts, n_dec * steps_per_seq
    n_steps = max(prompt_steps, sample_steps)

    def unit(t):
        return jnp.minimum(t // parts, n_units - 1)

    def seq_of(t):
        return jnp.minimum(t // steps_per_seq, n_dec - 1)

    def page_spec(slot):
        def index(t, pt):
            j = jnp.where(t < sample_steps, t % steps_per_seq, steps_per_seq - 1)
            return (layer, pt[seq_of(t), j * n_slots + slot], 0, 0, 0)
        return pl.BlockSpec((None, None, page, n_heads, HEAD_W), index)

    head_spec = pl.BlockSpec((seq, HEAD_W), lambda t, pt: (unit(t) // n_heads, unit(t) % n_heads))
    row_spec = pl.BlockSpec((dec_seq, d_model), lambda t, pt: (seq_of(t), 0))
    new_spec = pl.BlockSpec((None, dec_seq, n_heads, HEAD_W), lambda t, pt: (seq_of(t), 0, 0, 0))
    kern = functools.partial(_attn_kernel, n_slots=n_slots, n_heads=n_heads, dec_seq=dec_seq, tq=tq,
                             lam_init=lam_init, q_split=q_split, prompt_steps=prompt_steps,
                             sample_steps=sample_steps, steps_per_seq=steps_per_seq)
    grid_spec = pltpu.PrefetchScalarGridSpec(
        num_scalar_prefetch=1,
        grid=(n_steps,),
        in_specs=([head_spec, head_spec,
                   pl.BlockSpec((None, nq, HEAD_W, tq), lambda t, pt: (unit(t), 0, 0, 0)),
                   row_spec]
                  + [page_spec(s) for s in range(n_slots)] * 2 + [new_spec, new_spec]
                  + [pl.BlockSpec((4, HEAD_DIM), lambda t, pt: (0, 0)),
                     pl.BlockSpec((HEAD_W, 1), lambda t, pt: (0, 0))]),
        out_specs=[head_spec, row_spec],
        scratch_shapes=[
            pltpu.VMEM((2 * tq, HEAD_W), BF16),
            pltpu.VMEM((1, 2 * tq), F32),
            pltpu.VMEM((HEAD_W + DENOM_ROWS, 2 * tq), F32),
            pltpu.VMEM((1, 2 * tq), F32),
            pltpu.VMEM((1, 2 * tq), F32),
            pltpu.VMEM((2, 2 * tq // gw, tq, gw), F32),
            pltpu.VMEM((2, 2 * tq // gw, tq, gw), BF16),
            pltpu.VMEM((n_groups, width, HEAD_W), BF16),
            pltpu.VMEM((n_slots, n_groups, hp, width), F32),
            pltpu.VMEM((n_slots, n_groups, hp, width), F32),
            pltpu.VMEM((n_slots, n_groups, HEAD_W, width), F32),
        ],
    )
    new_rows = lambda a: a.reshape(n_dec, dec_seq, n_heads, HEAD_W)
    return pl.pallas_call(
        kern,
        grid_spec=grid_spec,
        out_shape=[jax.ShapeDtypeStruct((m_rows, d_model), BF16), jax.ShapeDtypeStruct((ms, d_model), F32)],
        compiler_params=_params("arbitrary"),
    )(page_table, qb, kb, vt, qs, *([cache_k] * n_slots), *([cache_v] * n_slots), new_rows(kn), new_rows(vn),
      lamv, subln_col)


def _tile(n, target):
    t = min(n, target)
    assert n % t == 0
    return t


def kernel(x_prompt, x_sample, cache_k, cache_v, page_table, gm_w_in, gm_b_in, gm_ln_g, gm_ln_b, gm_w_s, gm_b_s,
           gm_w_out, gm_b_out, at_w_qkv, at_lambda_q1, at_lambda_k1, at_lambda_q2, at_lambda_k2, at_subln_g,
           at_w_out, ln_mix_g, ln_mix_b, ln_ffn_g, ln_ffn_b, ffn_w_in, ffn_w_out):
    n_batch, seq, d_model = x_prompt.shape
    n_dec, dec_seq, _ = x_sample.shape
    depth = ln_mix_g.shape[0]
    n_heads = d_model // HEAD_W
    n_pages = page_table.shape[1]
    past_len = n_pages * cache_k.shape[2]
    alpha = (2 * depth) ** 0.25
    inner = gm_w_in.shape[2] // 2
    d_ff = ffn_w_out.shape[1]

    xp = x_prompt.reshape(n_batch * seq, d_model)
    xs = x_sample.reshape(n_dec * dec_seq, d_model)
    mp, ms = xp.shape[0], xs.shape[0]

    tm_p = _tile(seq, 1024)
    tm_r = _tile(seq, 512)
    tm_g = _tile(seq, 512)
    tn = _tile(inner, 512)
    tf = 256 if d_ff % 256 == 0 else d_ff

    tables_p = _rope_tables(jnp.arange(seq, dtype=jnp.int32))
    tables_s = _rope_tables(jnp.tile(past_len + jnp.arange(dec_seq, dtype=jnp.int32), n_dec))

    gm_v_p, gm_v_s, k_p, v_p, k_s, v_s = [], [], [], [], [], []
    for i in range(depth):
        j = i // 2
        if i % 2 == 0:
            bexp = jnp.repeat(gm_b_s[j].T, GROUP_W, axis=1)
            wexp = jnp.repeat(jnp.transpose(gm_w_s[j][:, :dec_seq, :dec_seq], (2, 1, 0)), GROUP_W, axis=2)
            gated_p, gv_p = _gmlp_in_prompt(xp, gm_w_in, gm_b_in, gm_ln_g, gm_ln_b, gm_w_s, bexp, j, seq,
                                            tm=tm_g, tn=tn)
            gated_s, gv_s = _gmlp_in_sample(xs, gm_w_in, gm_b_in, gm_ln_g, gm_ln_b, wexp, bexp[:dec_seq], j,
                                            dec_seq, tn=tn)
            gm_v_p.append(gv_p)
            gm_v_s.append(gv_s.reshape(n_dec, dec_seq, inner))
            mix_p = dict(a=gated_p, w=gm_w_out, bias=gm_b_out)
            mix_s = dict(a=gated_s, w=gm_w_out, bias=gm_b_out)
        else:
            lam_init = 0.8 - 0.6 * math.exp(-0.3 * i)
            lamv = jnp.stack([at_lambda_q1[j], at_lambda_k1[j], at_lambda_q2[j], at_lambda_k2[j]])
            subln = at_subln_g[j].reshape(HEAD_W, 1)
            scale = HEAD_DIM ** -0.5 * math.log2(math.e)
            tq = _tile(seq, 512)
            proj = functools.partial(_proj_rope, xp, xs, at_w_qkv, j, n_cols=d_model, tm=tm_r)
            qb, qs = proj(col0=0, tables=tables_p, tables_s=tables_s, pos_blocks=seq // tm_r,
                          out_dtypes=(BF16,), scale=scale)
            kf, kb, kn = proj(col0=d_model, tables=tables_p, tables_s=tables_s, pos_blocks=seq // tm_r,
                              out_dtypes=(F32, BF16), scale=1.0)
            vf, vn, vt = proj(col0=2 * d_model, tables=None, tables_s=None, pos_blocks=1,
                              out_dtypes=(F32,), scale=1.0, vt=(seq, tq))
            a_p, a_s = _attention(qb, kb, vt, qs, cache_k, cache_v, page_table, kn, vn, lamv, subln, j, n_batch, seq,
                                  dec_seq, lam_init=lam_init, tq=tq, n_slots=math.gcd(n_pages, 8))
            k_p.append(kf.reshape(n_batch, seq, n_heads, HEAD_W))
            v_p.append(vf.reshape(n_batch, seq, n_heads, HEAD_W))
            k_s.append(kn.reshape(n_dec, dec_seq, n_heads, HEAD_W))
            v_s.append(vn.reshape(n_dec, dec_seq, n_heads, HEAD_W))
            mix_p = dict(a=a_p, w=at_w_out, bias=None)
            mix_s = dict(a=a_s, w=at_w_out, bias=None)
        xp, xs = _proj_ln(mix_p["a"], mix_s["a"], mix_p["w"], j, mix_p["bias"], xp, xs, ln_mix_g, ln_mix_b, i,
                          alpha=alpha, tm=tm_r)
        xp, xs = _ffn(xp, xs, ffn_w_in, ffn_w_out, i, ln_ffn_g, ln_ffn_b, alpha=alpha, tm=tm_p, tf=tf)

    return (xp.reshape(n_batch, seq, d_model), xs.reshape(n_dec, dec_seq, d_model),
            jnp.stack(gm_v_p), jnp.stack(gm_v_s), jnp.stack(k_p), jnp.stack(v_p), jnp.stack(k_s), jnp.stack(v_s))
```

```python
import functools
import math

import jax
import jax.numpy as jnp
from jax import lax
from jax.experimental import pallas as pl
from jax.experimental.pallas import tpu as pltpu

F32 = jnp.float32
BF16 = jnp.bfloat16

LN_EPS = 1e-5
CHUNK = 128
GROUP_W = 128
HEAD_DIM = 64
HEAD_W = 2 * HEAD_DIM
ROT_DIM = HEAD_DIM // 4
ROPE_THETA = 500000.0
LANES = 128
SUBLANES = 8
MXU_COLS = 256
DENOM_ROWS = 16
LN_ROWS = 128
SOFTMAX_ROWS = 64
VMEM_LIMIT_BYTES = 60 * 1024 * 1024


def _params(*sem):
    return pltpu.CompilerParams(dimension_semantics=sem, vmem_limit_bytes=VMEM_LIMIT_BYTES)


def _layer_norm(y, g, b):
    mu = jnp.mean(y, axis=-1, keepdims=True)
    yc = y - mu
    var = jnp.mean(yc * yc, axis=-1, keepdims=True)
    return yc * lax.rsqrt(var + LN_EPS) * g + b


def _residual_ln_inplace(o_ref, x_ref, bias_ref, g_ref, b_ref, alpha, tm):
    block = min(tm, LN_ROWS)
    assert tm % block == 0

    def body(r, carry):
        rows = pl.ds(pl.multiple_of(r * block, block), block)
        y = alpha * x_ref[rows, :] + o_ref[rows, :]
        if bias_ref is not None:
            y = y + bias_ref[...]
        o_ref[rows, :] = _layer_norm(y, g_ref[...], b_ref[...])
        return carry

    lax.fori_loop(0, tm // block, body, 0)


def _cast_rows_to_bf16(dst_ref, src_ref):
    rows = 256
    n = src_ref.shape[0]
    assert n % rows == 0

    def body(r, carry):
        sl = pl.ds(pl.multiple_of(r * rows, rows), rows)
        dst_ref[sl, :] = src_ref[sl, :].astype(BF16)
        return carry

    lax.fori_loop(0, n // rows, body, 0)


def _gmlp_project(x_ref, w_ref, b_ref, xb_s, z_s, wb_s=None):
    j = pl.program_id(1)

    @pl.when(j == 0)
    def _():
        xb_s[...] = x_ref[...].astype(BF16)

    if wb_s is None:
        wb = w_ref[...].astype(BF16)
    else:
        @pl.when(pl.program_id(0) == 0)
        def _():
            wb_s[j] = w_ref[...].astype(BF16)

        wb = wb_s[j]
    z = jnp.dot(xb_s[...], wb, preferred_element_type=F32) + b_ref[...]
    z_s[j] = 0.5 * z * (1.0 + lax.erf(z * math.sqrt(0.5)))


def _gmlp_v_layer_norm(z_s, lng_ref, lnb_ref, rows, n_half, tn, inner):
    vs = [z_s[n_half + c, rows, :] for c in range(n_half)]
    mu = sum(jnp.sum(v, axis=-1, keepdims=True) for v in vs) / inner
    var = sum(jnp.sum(jnp.square(v - mu), axis=-1, keepdims=True) for v in vs) / inner
    rstd = lax.rsqrt(var + LN_EPS)
    return [(vs[c] - mu) * rstd * lng_ref[:, c * tn:(c + 1) * tn] + lnb_ref[:, c * tn:(c + 1) * tn]
            for c in range(n_half)]


def _gmlp_in_prompt_kernel(x_ref, w_ref, b_ref, lng_ref, lnb_ref, ws_ref, bexp_ref,
                           gated_ref, gmv_ref, xb_s, z_s, vn_s, wb_s, *, tm, tn, inner):
    _gmlp_project(x_ref, w_ref, b_ref, xb_s, z_s, wb_s)
    n_half = inner // tn
    n_groups = inner // GROUP_W
    n_rc = tm // CHUNK
    per_chunk = tn // GROUP_W

    @pl.when(pl.program_id(1) == pl.num_programs(1) - 1)
    def _():
        for r in range(n_rc):
            rows = slice(r * CHUNK, (r + 1) * CHUNK)
            vn = _gmlp_v_layer_norm(z_s, lng_ref, lnb_ref, rows, n_half, tn, inner)
            for c in range(n_half):
                if r == n_rc - 1:
                    gmv_ref[:, c * tn:(c + 1) * tn] = vn[c]
                for q in range(per_chunk):
                    g = c * per_chunk + q
                    vn_s[g, :, r * CHUNK:(r + 1) * CHUNK] = vn[c][:, q * GROUP_W:(q + 1) * GROUP_W].astype(BF16)
        t_idx = lax.broadcasted_iota(jnp.int32, (CHUNK, CHUNK), 0)
        s_idx = lax.broadcasted_iota(jnp.int32, (CHUNK, CHUNK), 1)
        causal = s_idx <= t_idx
        for g in range(n_groups):
            w_causal = jnp.where(causal, ws_ref[g], 0.0).astype(BF16)
            mixed = jnp.dot(w_causal, vn_s[g], preferred_element_type=F32)
            cols = slice(g * GROUP_W, (g + 1) * GROUP_W)
            c, q = divmod(g, per_chunk)
            for r in range(n_rc):
                rows = slice(r * CHUNK, (r + 1) * CHUNK)
                u = z_s[c, rows, q * GROUP_W:(q + 1) * GROUP_W]
                m = mixed[:, r * CHUNK:(r + 1) * CHUNK] + bexp_ref[:, cols]
                gated_ref[rows, cols] = (u * m).astype(BF16)


def _gmlp_in_prompt(x, w_in, b_in, ln_g, ln_b, w_s, bexp, layer, seq, *, tm, tn):
    m_rows, d_model = x.shape
    inner = w_in.shape[2] // 2
    n_groups = inner // GROUP_W
    n_batch = m_rows // seq
    assert seq % tm == 0 and tm % CHUNK == 0 and inner % tn == 0 and tn % GROUP_W == 0
    tiles_per_seq = seq // tm
    nj = 2 * inner // tn
    kern = functools.partial(_gmlp_in_prompt_kernel, tm=tm, tn=tn, inner=inner)
    return pl.pallas_call(
        kern,
        grid=(m_rows // tm, nj),
        in_specs=[
            pl.BlockSpec((tm, d_model), lambda i, j: (i, 0)),
            pl.BlockSpec((None, d_model, tn), lambda i, j: (layer, 0, jnp.where(i == 0, j, nj - 1))),
            pl.BlockSpec((None, 1, tn), lambda i, j: (layer, 0, j)),
            pl.BlockSpec((None, 1, inner), lambda i, j: (layer, 0, 0)),
            pl.BlockSpec((None, 1, inner), lambda i, j: (layer, 0, 0)),
            pl.BlockSpec((None, n_groups, CHUNK, CHUNK), lambda i, j: (layer, 0, 0, 0)),
            pl.BlockSpec((CHUNK, inner), lambda i, j: (0, 0)),
        ],
        out_specs=[
            pl.BlockSpec((tm, inner), lambda i, j: (i, 0)),
            pl.BlockSpec((None, CHUNK, inner), lambda i, j: (i // tiles_per_seq, 0, 0)),
        ],
        out_shape=[
            jax.ShapeDtypeStruct((m_rows, inner), BF16),
            jax.ShapeDtypeStruct((n_batch, CHUNK, inner), F32),
        ],
        scratch_shapes=[
            pltpu.VMEM((tm, d_model), BF16),
            pltpu.VMEM((nj, tm, tn), F32),
            pltpu.VMEM((n_groups, CHUNK, tm), BF16),
            pltpu.VMEM((nj, d_model, tn), BF16),
        ],
        compiler_params=_params("arbitrary", "arbitrary"),
    )(x, w_in, b_in.reshape(b_in.shape[0], 1, -1), ln_g.reshape(ln_g.shape[0], 1, -1),
      ln_b.reshape(ln_b.shape[0], 1, -1), w_s, bexp)


def _gmlp_in_sample_kernel(x_ref, w_ref, b_ref, lng_ref, lnb_ref, wexp_ref, bexp_ref,
                           gated_ref, gmv_ref, xb_s, z_s, *, tm, tn, inner, dec_seq):
    _gmlp_project(x_ref, w_ref, b_ref, xb_s, z_s)
    n_half = inner // tn

    @pl.when(pl.program_id(1) == pl.num_programs(1) - 1)
    def _():
        vn = _gmlp_v_layer_norm(z_s, lng_ref, lnb_ref, slice(0, tm), n_half, tn, inner)
        t_idx = lax.broadcasted_iota(jnp.int32, (dec_seq, tn), 0)
        for c in range(n_half):
            cols = slice(c * tn, (c + 1) * tn)
            gmv_ref[:, cols] = vn[c]
            for b in range(tm // dec_seq):
                rows = slice(b * dec_seq, (b + 1) * dec_seq)
                vb = vn[c][rows, :]
                mixed = bexp_ref[:, cols]
                for s in range(dec_seq):
                    w_ts = jnp.where(t_idx >= s, wexp_ref[s, :, cols], 0.0)
                    mixed = mixed + w_ts * vb[s:s + 1, :]
                gated_ref[rows, cols] = (z_s[c, rows, :] * mixed).astype(BF16)


def _gmlp_in_sample(x, w_in, b_in, ln_g, ln_b, wexp, bexp, layer, dec_seq, *, tn):
    m_rows, d_model = x.shape
    inner = w_in.shape[2] // 2
    assert dec_seq % 8 == 0 and dec_seq <= CHUNK and inner % tn == 0
    nj = 2 * inner // tn
    kern = functools.partial(_gmlp_in_sample_kernel, tm=m_rows, tn=tn, inner=inner, dec_seq=dec_seq)
    return pl.pallas_call(
        kern,
        grid=(1, nj),
        in_specs=[
            pl.BlockSpec((m_rows, d_model), lambda i, j: (0, 0)),
            pl.BlockSpec((None, d_model, tn), lambda i, j: (layer, 0, j)),
            pl.BlockSpec((None, 1, tn), lambda i, j: (layer, 0, j)),
            pl.BlockSpec((None, 1, inner), lambda i, j: (layer, 0, 0)),
            pl.BlockSpec((None, 1, inner), lambda i, j: (layer, 0, 0)),
            pl.BlockSpec((dec_seq, dec_seq, inner), lambda i, j: (0, 0, 0)),
            pl.BlockSpec((dec_seq, inner), lambda i, j: (0, 0)),
        ],
        out_specs=[
            pl.BlockSpec((m_rows, inner), lambda i, j: (0, 0)),
            pl.BlockSpec((m_rows, inner), lambda i, j: (0, 0)),
        ],
        out_shape=[
            jax.ShapeDtypeStruct((m_rows, inner), BF16),
            jax.ShapeDtypeStruct((m_rows, inner), F32),
        ],
        scratch_shapes=[
            pltpu.VMEM((m_rows, d_model), BF16),
            pltpu.VMEM((nj, m_rows, tn), F32),
        ],
        compiler_params=_params("arbitrary", "arbitrary"),
    )(x, w_in, b_in.reshape(b_in.shape[0], 1, -1), ln_g.reshape(ln_g.shape[0], 1, -1),
      ln_b.reshape(ln_b.shape[0], 1, -1), wexp, bexp)


def _proj_ln_kernel(*refs, alpha, has_bias):
    if has_bias:
        a_ref, as_ref, w_ref, bias_ref, x_ref, xs_ref, g_ref, b_ref, o_ref, os_ref, wb_s = refs
    else:
        a_ref, as_ref, w_ref, x_ref, xs_ref, g_ref, b_ref, o_ref, os_ref, wb_s = refs
        bias_ref = None

    def apply(a, x_in, o):
        o[...] = jnp.dot(a[...].astype(BF16), wb_s[...], preferred_element_type=F32)
        _residual_ln_inplace(o, x_in, bias_ref, g_ref, b_ref, alpha, o.shape[0])

    @pl.when(pl.program_id(0) == 0)
    def _():
        _cast_rows_to_bf16(wb_s, w_ref)
        apply(as_ref, xs_ref, os_ref)

    apply(a_ref, x_ref, o_ref)


def _proj_ln(a, a_s, w, layer, bias, x, xs, ln_g, ln_b, ln_idx, *, alpha, tm):
    m_rows, k_dim = a.shape
    ms = a_s.shape[0]
    d_model = x.shape[1]
    assert m_rows % tm == 0
    vec = lambda idx: pl.BlockSpec((None, 1, d_model), lambda i: (idx, 0, 0))
    in_specs = [pl.BlockSpec((tm, k_dim), lambda i: (i, 0)),
                pl.BlockSpec((ms, k_dim), lambda i: (0, 0)),
                pl.BlockSpec((None, k_dim, d_model), lambda i: (layer, 0, 0), pipeline_mode=pl.Buffered(1))]
    args = [a, a_s, w]
    if bias is not None:
        in_specs.append(vec(layer))
        args.append(bias.reshape(bias.shape[0], 1, -1))
    in_specs += [pl.BlockSpec((tm, d_model), lambda i: (i, 0)), pl.BlockSpec((ms, d_model), lambda i: (0, 0)),
                 vec(ln_idx), vec(ln_idx)]
    args += [x, xs, ln_g.reshape(ln_g.shape[0], 1, -1), ln_b.reshape(ln_b.shape[0], 1, -1)]
    kern = functools.partial(_proj_ln_kernel, alpha=alpha, has_bias=bias is not None)
    return pl.pallas_call(
        kern,
        grid=(m_rows // tm,),
        in_specs=in_specs,
        out_specs=[pl.BlockSpec((tm, d_model), lambda i: (i, 0)), pl.BlockSpec((ms, d_model), lambda i: (0, 0))],
        out_shape=[jax.ShapeDtypeStruct((m_rows, d_model), F32), jax.ShapeDtypeStruct((ms, d_model), F32)],
        scratch_shapes=[pltpu.VMEM((k_dim, d_model), BF16)],
        compiler_params=_params("arbitrary"),
    )(*args)


def _ffn_kernel(x_ref, xs_ref, wg_ref, wu_ref, wo_ref, g_ref, b_ref, o_ref, os_ref, xb_s, *, alpha, tm):
    i, f = pl.program_id(0), pl.program_id(1)
    last = pl.num_programs(1) - 1

    @pl.when(f == 0)
    def _():
        xb_s[0:tm, :] = x_ref[...].astype(BF16)
        o_ref[...] = jnp.zeros_like(o_ref)

    @pl.when(jnp.logical_and(f == 0, i == 0))
    def _():
        xb_s[tm:, :] = xs_ref[...].astype(BF16)
        os_ref[...] = jnp.zeros_like(os_ref)

    def swiglu(xb):
        gate = jnp.dot(xb, wg_ref[...].astype(BF16), preferred_element_type=F32)
        up = jnp.dot(xb, wu_ref[...].astype(BF16), preferred_element_type=F32)
        h = (jax.nn.silu(gate) * up).astype(BF16)
        return jnp.dot(h, wo_ref[...].astype(BF16), preferred_element_type=F32)

    @pl.when(i == 0)
    def _():
        part = swiglu(xb_s[...])
        o_ref[...] += part[0:tm]
        os_ref[...] += part[tm:]

    @pl.when(i > 0)
    def _():
        o_ref[...] += swiglu(xb_s[0:tm, :])

    @pl.when(f == last)
    def _():
        _residual_ln_inplace(o_ref, x_ref, None, g_ref, b_ref, alpha, tm)

    @pl.when(jnp.logical_and(f == last, i == 0))
    def _():
        _residual_ln_inplace(os_ref, xs_ref, None, g_ref, b_ref, alpha, os_ref.shape[0])


def _ffn(x, xs, w_in, w_out, layer, ln_g, ln_b, *, alpha, tm, tf):
    m_rows, d_model = x.shape
    ms = xs.shape[0]
    d_ff = w_out.shape[1]
    assert m_rows % tm == 0 and d_ff % tf == 0
    nf = d_ff // tf
    vec = pl.BlockSpec((None, 1, d_model), lambda i, f: (layer, 0, 0))
    kern = functools.partial(_ffn_kernel, alpha=alpha, tm=tm)
    return pl.pallas_call(
        kern,
        grid=(m_rows // tm, nf),
        in_specs=[
            pl.BlockSpec((tm, d_model), lambda i, f: (i, 0)),
            pl.BlockSpec((ms, d_model), lambda i, f: (0, 0)),
            pl.BlockSpec((None, d_model, tf), lambda i, f: (layer, 0, f)),
            pl.BlockSpec((None, d_model, tf), lambda i, f: (layer, 0, nf + f)),
            pl.BlockSpec((None, tf, d_model), lambda i, f: (layer, f, 0)),
            vec, vec,
        ],
        out_specs=[pl.BlockSpec((tm, d_model), lambda i, f: (i, 0)),
                   pl.BlockSpec((ms, d_model), lambda i, f: (0, 0))],
        out_shape=[jax.ShapeDtypeStruct((m_rows, d_model), F32), jax.ShapeDtypeStruct((ms, d_model), F32)],
        scratch_shapes=[pltpu.VMEM((tm + ms, d_model), BF16)],
        compiler_params=_params("arbitrary", "arbitrary"),
    )(x, xs, w_in, w_in, w_out, ln_g.reshape(ln_g.shape[0], 1, -1), ln_b.reshape(ln_b.shape[0], 1, -1))


def _rope_tables(pos):
    half = ROT_DIM // 2
    inv = jnp.power(ROPE_THETA, -jnp.arange(half, dtype=F32) * 2.0 / ROT_DIM)
    ang = pos.astype(F32)[:, None] * inv[None, :]
    cos, sin = jnp.cos(ang), jnp.sin(ang)
    n = pos.shape[0]
    rest = HEAD_DIM - ROT_DIM
    c = jnp.concatenate([cos, cos, jnp.ones((n, rest), F32)], axis=1)
    s_next = jnp.concatenate([-sin, jnp.zeros((n, half + rest), F32)], axis=1)
    s_prev = jnp.concatenate([jnp.zeros((n, half), F32), sin, jnp.zeros((n, rest), F32)], axis=1)
    return tuple(jnp.tile(t, (1, HEAD_W // HEAD_DIM)) for t in (c, s_next, s_prev))


def _proj_rope_kernel(*refs, rope, scale, n_out, vt_tk, n_after):
    n_tab = 3 if rope else 0
    x_ref, xs_ref, w_ref = refs[:3]
    tabs, tabs_s = refs[3:3 + n_tab], refs[3 + n_tab:3 + 2 * n_tab]
    k = 3 + 2 * n_tab + n_after
    out_refs, os_ref = refs[k:k + n_out], refs[k + n_out]
    vt_ref = refs[k + n_out + 1] if vt_tk else None
    wb_s = refs[-1]

    def emit(x_in, tables, outs, vt):
        y = jnp.dot(x_in[...].astype(BF16), wb_s[...], preferred_element_type=F32)
        for h in range(y.shape[1] // HEAD_W):
            cols = slice(h * HEAD_W, (h + 1) * HEAD_W)
            yh = y[:, cols]
            if rope:
                c_ref, sn_ref, sp_ref = tables
                half = ROT_DIM // 2
                yh = (yh * c_ref[...] + pltpu.roll(yh, HEAD_W - half, 1) * sn_ref[...]
                      + pltpu.roll(yh, half, 1) * sp_ref[...])
            if scale != 1.0:
                yh = yh * scale
            for o_ref in outs:
                o_ref[:, cols] = yh.astype(o_ref.dtype)
            if vt is not None:
                for kb in range(y.shape[0] // vt_tk):
                    vt[h, kb] = yh[kb * vt_tk:(kb + 1) * vt_tk, :].T.astype(BF16)

    @pl.when(pl.program_id(0) == 0)
    def _():
        _cast_rows_to_bf16(wb_s, w_ref)
        emit(xs_ref, tabs_s, (os_ref,), None)

    emit(x_ref, tabs, out_refs, vt_ref)


def _proj_rope(x, xs, w, layer, col0, n_cols, tables, tables_s, pos_blocks, out_dtypes, *, scale, tm, vt=None,
               after=None):
    m_rows, d_model = x.shape
    ms = xs.shape[0]
    assert m_rows % tm == 0 and col0 % n_cols == 0 and n_cols % HEAD_W == 0
    rope = tables is not None
    in_specs = [pl.BlockSpec((tm, d_model), lambda i: (i, 0)),
                pl.BlockSpec((ms, d_model), lambda i: (0, 0)),
                pl.BlockSpec((None, d_model, n_cols), lambda i: (layer, 0, col0 // n_cols),
                             pipeline_mode=pl.Buffered(1))]
    args = [x, xs, w]
    if rope:
        in_specs += [pl.BlockSpec((tm, HEAD_W), lambda i: (i % pos_blocks, 0))] * 3
        in_specs += [pl.BlockSpec((ms, HEAD_W), lambda i: (0, 0))] * 3
        args += list(tables) + list(tables_s)
    if after is not None:
        in_specs.append(pl.BlockSpec(memory_space=pl.ANY))
        args.append(after)
    out_specs = [pl.BlockSpec((tm, n_cols), lambda i: (i, 0)) for _ in out_dtypes]
    out_shape = [jax.ShapeDtypeStruct((m_rows, n_cols), dt) for dt in out_dtypes]
    out_specs.append(pl.BlockSpec((ms, n_cols), lambda i: (0, 0)))
    out_shape.append(jax.ShapeDtypeStruct((ms, n_cols), F32))
    vt_tk = 0
    if vt is not None:
        seq, vt_tk = vt
        assert seq % tm == 0 and tm % vt_tk == 0
        tiles_per_seq, n_heads = seq // tm, n_cols // HEAD_W
        out_specs.append(pl.BlockSpec((n_heads, tm // vt_tk, HEAD_W, vt_tk),
                                      lambda i: (i // tiles_per_seq, i % tiles_per_seq, 0, 0)))
        out_shape.append(jax.ShapeDtypeStruct(((m_rows // seq) * n_heads, seq // vt_tk, HEAD_W, vt_tk), BF16))
    kern = functools.partial(_proj_rope_kernel, rope=rope, scale=scale, n_out=len(out_dtypes), vt_tk=vt_tk,
                             n_after=0 if after is None else 1)
    return pl.pallas_call(
        kern,
        grid=(m_rows // tm,),
        in_specs=in_specs,
        out_specs=out_specs,
        out_shape=out_shape,
        scratch_shapes=[pltpu.VMEM((d_model, n_cols), BF16)],
        compiler_params=_params("arbitrary"),
    )(*args)


def _diff_lambda(lam_ref, lam_init):
    lv = lam_ref[...]
    e1 = jnp.exp(jnp.sum(lv[0:1, :] * lv[1:2, :], axis=-1, keepdims=True))
    e2 = jnp.exp(jnp.sum(lv[2:3, :] * lv[3:4, :], axis=-1, keepdims=True))
    return e1 - e2 + lam_init


def _map_masks(rows):
    lane = lax.broadcasted_iota(jnp.int32, (rows, HEAD_W), 1)
    return lane < HEAD_DIM, lane >= HEAD_DIM


def _sub_ln(o, g_col, lam_init):
    return o * lax.rsqrt(jnp.mean(o * o, axis=0, keepdims=True) + LN_EPS) * g_col * (1.0 - lam_init)


def _prompt_attention(q_ref, k_ref, vt_ref, lam_ref, g_ref, o_ref, q2_s, m_s, acc_s, c_s, mx_s, s_s, p_s,
                      *, tq, lam_init, q_steps):
    gw = min(MXU_COLS, tq)
    n_g = 2 * tq // gw
    m0, m1 = _map_masks(tq)
    lam = _diff_lambda(lam_ref, lam_init)

    def block(ki, diagonal, buf):
        kb = k_ref[ki * tq:(ki + 1) * tq, :]
        vtb = jnp.concatenate([vt_ref[ki], jnp.ones((DENOM_ROWS, tq), BF16)], axis=0)
        n_keys = [min(tq, (g * gw) % tq + gw) if diagonal else tq for g in range(n_g)]
        for g in range(n_g):
            cols, nk = slice(g * gw, (g + 1) * gw), n_keys[g]
            s = lax.dot_general(kb[:nk], q2_s[cols, :], (((1,), (1,)), ((), ())), preferred_element_type=F32)
            if diagonal:
                key = lax.broadcasted_iota(jnp.int32, (nk, gw), 0)
                qry = lax.broadcasted_iota(jnp.int32, (nk, gw), 1) + (g * gw) % tq
                s = jnp.where(key <= qry, s, -jnp.inf)
            s_s[buf, g, 0:nk, :] = s
            mx_s[:, cols] = jnp.max(s, axis=0, keepdims=True)
        for g in range(n_g):
            cols, nk = slice(g * gw, (g + 1) * gw), n_keys[g]
            m_old = m_s[:, cols]
            m_new = jnp.maximum(m_old, mx_s[:, cols])
            c_s[:, cols] = jnp.exp2(m_old - m_new)
            for r in range(0, nk, SOFTMAX_ROWS):
                p_s[buf, g, r:r + SOFTMAX_ROWS, :] = jnp.exp2(s_s[buf, g, r:r + SOFTMAX_ROWS, :] - m_new).astype(BF16)
            m_s[:, cols] = m_new
        for g in range(n_g):
            cols, nk = slice(g * gw, (g + 1) * gw), n_keys[g]
            acc_s[:, cols] = c_s[:, cols] * acc_s[:, cols] + jnp.dot(vtb[:, :nk], p_s[buf, g, 0:nk, :],
                                                                     preferred_element_type=F32)

    n_blocks = 0
    for qi in q_steps:
        rows = slice(qi * tq, (qi + 1) * tq)
        q = q_ref[rows, :]
        zero = jnp.zeros_like(q)
        q2_s[0:tq, :] = jnp.where(m0, q, zero)
        q2_s[tq:2 * tq, :] = jnp.where(m1, q, zero)
        m_s[...] = jnp.full_like(m_s, -jnp.inf)
        acc_s[...] = jnp.zeros_like(acc_s)
        for ki in range(qi + 1):
            block(ki, ki == qi, n_blocks % 2)
            n_blocks += 1
        o = acc_s[0:HEAD_W, :] / acc_s[HEAD_W:HEAD_W + 1, :]
        o = o[:, :tq] - lam * o[:, tq:]
        o_ref[rows, :] = _sub_ln(o, g_ref[...], lam_init).T.astype(o_ref.dtype)


def _sample_attention(j, n_steps, q_ref, k_pages, v_pages, kn_ref, vn_ref, lam_ref, g_ref, o_ref,
                      qm_s, m_s, l_s, acc_s, *, n_heads, dec_seq, lam_init):
    n_slots = len(k_pages)
    hp = min(n_heads, SUBLANES)
    n_groups = n_heads // hp
    cw = 2 * dec_seq
    width = hp * cw

    @pl.when(j == 0)
    def _():
        m0, m1 = _map_masks(dec_seq)
        for h in range(n_heads):
            grp, hl = divmod(h, hp)
            qh = q_ref[:, h * HEAD_W:(h + 1) * HEAD_W]
            qm_s[grp, hl * cw:(hl + 1) * cw, :] = jnp.concatenate(
                [jnp.where(m0, qh, 0.0), jnp.where(m1, qh, 0.0)], axis=0).astype(BF16)
        m_s[...] = jnp.full_like(m_s, -jnp.inf)
        l_s[...] = jnp.zeros_like(l_s)
        acc_s[...] = jnp.zeros_like(acc_s)

    own = (lax.broadcasted_iota(jnp.int32, (hp, width), 1) // cw
           == lax.broadcasted_iota(jnp.int32, (hp, width), 0))

    def update(slot, k3_ref, v3_ref, causal):
        n_keys = k3_ref.shape[0]
        rows = n_keys * hp
        for grp in range(n_groups):
            heads = slice(grp * hp, (grp + 1) * hp)
            kr = k3_ref[:, heads, :].reshape(rows, HEAD_W).astype(BF16)
            u = lax.dot_general(kr, qm_s[grp], (((1,), (1,)), ((), ())), preferred_element_type=F32)
            u = u.reshape(n_keys, hp, width)
            if causal:
                key = lax.broadcasted_iota(jnp.int32, (n_keys, hp, width), 0)
                qry = lax.broadcasted_iota(jnp.int32, (n_keys, hp, width), 2) % dec_seq
                u = jnp.where(key <= qry, u, -jnp.inf)
            m_old = m_s[slot, grp]
            m_new = jnp.maximum(m_old, jnp.max(u, axis=0))
            p = jnp.where(own, jnp.exp2(u - m_new), 0.0)
            corr = jnp.exp2(m_old - m_new)
            l_s[slot, grp] = corr * l_s[slot, grp] + jnp.sum(p, axis=0)
            m_s[slot, grp] = m_new
            vr = v3_ref[:, heads, :].reshape(rows, HEAD_W).astype(BF16)
            pv = lax.dot_general(vr, p.reshape(rows, width).astype(BF16), (((0,), (0,)), ((), ())),
                                 preferred_element_type=F32)
            corr_col = jnp.sum(jnp.where(own, corr, 0.0), axis=0, keepdims=True)
            acc_s[slot, grp] = acc_s[slot, grp] * corr_col + pv

    for slot in range(n_slots):
        update(slot, k_pages[slot], v_pages[slot], False)

    @pl.when(j == n_steps - 1)
    def _():
        update(0, kn_ref, vn_ref, True)
        lam = _diff_lambda(lam_ref, lam_init)
        for grp in range(n_groups):
            m_all = m_s[0, grp]
            for slot in range(1, n_slots):
                m_all = jnp.maximum(m_all, m_s[slot, grp])
            l_all = jnp.zeros((hp, width), F32)
            acc = jnp.zeros((HEAD_W, width), F32)
            for slot in range(n_slots):
                w = jnp.exp2(m_s[slot, grp] - m_all)
                l_all = l_all + w * l_s[slot, grp]
                acc = acc + acc_s[slot, grp] * jnp.sum(jnp.where(own, w, 0.0), axis=0, keepdims=True)
            o = acc / jnp.sum(jnp.where(own, l_all, 0.0), axis=0, keepdims=True)
            o = o - lam * pltpu.roll(o, width - dec_seq, 1)
            ot = _sub_ln(o, g_ref[...], lam_init).T
            for hl in range(hp):
                h = grp * hp + hl
                o_ref[:, h * HEAD_W:(h + 1) * HEAD_W] = ot[hl * cw:hl * cw + dec_seq, :]


def _attn_kernel(pt_ref, qp_ref, kp_ref, vtp_ref, qs_ref, *refs, n_slots, n_heads, dec_seq, tq, lam_init,
                 q_split, prompt_steps, sample_steps, steps_per_seq):
    del pt_ref
    k_pages, v_pages = refs[:n_slots], refs[n_slots:2 * n_slots]
    kn_ref, vn_ref, lam_ref, g_ref, op_ref, os_ref = refs[2 * n_slots:2 * n_slots + 6]
    prompt_scratch, sample_scratch = refs[2 * n_slots + 6:-4], refs[-4:]
    t = pl.program_id(0)

    for part, q_steps in enumerate(q_split):
        @pl.when(jnp.logical_and(t < prompt_steps, t % len(q_split) == part))
        def _():
            _prompt_attention(qp_ref, kp_ref, vtp_ref, lam_ref, g_ref, op_ref, *prompt_scratch,
                              tq=tq, lam_init=lam_init, q_steps=q_steps)

    @pl.when(t < sample_steps)
    def _():
        _sample_attention(t % steps_per_seq, steps_per_seq, qs_ref, k_pages, v_pages, kn_ref, vn_ref, lam_ref, g_ref,
                          os_ref, *sample_scratch, n_heads=n_heads, dec_seq=dec_seq, lam_init=lam_init)


def _attention(qb, kb, vt, qs, cache_k, cache_v, page_table, kn, vn, lamv, subln_col, layer, n_batch, seq, dec_seq,
               *, lam_init, tq, n_slots):
    m_rows, d_model = qb.shape
    ms = qs.shape[0]
    n_heads = d_model // HEAD_W
    n_dec, n_pages = page_table.shape
    page = cache_k.shape[2]
    hp = min(n_heads, SUBLANES)
    nq = seq // tq
    gw = min(MXU_COLS, tq)
    assert seq % tq == 0 and tq % SOFTMAX_ROWS == 0 and vt.shape[1:] == (nq, HEAD_W, tq)
    assert n_pages % n_slots == 0 and dec_seq % SUBLANES == 0 and n_heads % hp == 0
    width = hp * 2 * dec_seq
    n_groups = n_heads // hp
    q_split = [tuple(range(nq - 1)), (nq - 1,)] if nq > 1 else [(0,)]
    n_units, parts = n_batch * n_heads, len(q_split)
    steps_per_seq = n_pages // n_slots
    prompt_steps, sample_steps = n_units * parts, n_dec * steps_per_seq
    n_steps = max(prompt_steps, sample_steps)

    def unit(t):
        return jnp.minimum(t // parts, n_units - 1)

    def seq_of(t):
        return jnp.minimum(t // steps_per_seq, n_dec - 1)

    def page_spec(slot):
        def index(t, pt):
            j = jnp.where(t < sample_steps, t % steps_per_seq, steps_per_seq - 1)
            return (layer, pt[seq_of(t), j * n_slots + slot], 0, 0, 0)
        return pl.BlockSpec((None, None, page, n_heads, HEAD_W), index)

    head_spec = pl.BlockSpec((seq, HEAD_W), lambda t, pt: (unit(t) // n_heads, unit(t) % n_heads))
    row_spec = pl.BlockSpec((dec_seq, d_model), lambda t, pt: (seq_of(t), 0))
    new_spec = pl.BlockSpec((None, dec_seq, n_heads, HEAD_W), lambda t, pt: (seq_of(t), 0, 0, 0))
    kern = functools.partial(_attn_kernel, n_slots=n_slots, n_heads=n_heads, dec_seq=dec_seq, tq=tq,
                             lam_init=lam_init, q_split=q_split, prompt_steps=prompt_steps,
                             sample_steps=sample_steps, steps_per_seq=steps_per_seq)
    grid_spec = pltpu.PrefetchScalarGridSpec(
        num_scalar_prefetch=1,
        grid=(n_steps,),
        in_specs=([head_spec, head_spec,
                   pl.BlockSpec((None, nq, HEAD_W, tq), lambda t, pt: (unit(t), 0, 0, 0)),
                   row_spec]
                  + [page_spec(s) for s in range(n_slots)] * 2 + [new_spec, new_spec]
                  + [pl.BlockSpec((4, HEAD_DIM), lambda t, pt: (0, 0)),
                     pl.BlockSpec((HEAD_W, 1), lambda t, pt: (0, 0))]),
        out_specs=[head_spec, row_spec],
        scratch_shapes=[
            pltpu.VMEM((2 * tq, HEAD_W), BF16),
            pltpu.VMEM((1, 2 * tq), F32),
            pltpu.VMEM((HEAD_W + DENOM_ROWS, 2 * tq), F32),
            pltpu.VMEM((1, 2 * tq), F32),
            pltpu.VMEM((1, 2 * tq), F32),
            pltpu.VMEM((2, 2 * tq // gw, tq, gw), F32),
            pltpu.VMEM((2, 2 * tq // gw, tq, gw), BF16),
            pltpu.VMEM((n_groups, width, HEAD_W), BF16),
            pltpu.VMEM((n_slots, n_groups, hp, width), F32),
            pltpu.VMEM((n_slots, n_groups, hp, width), F32),
            pltpu.VMEM((n_slots, n_groups, HEAD_W, width), F32),
        ],
    )
    new_rows = lambda a: a.reshape(n_dec, dec_seq, n_heads, HEAD_W)
    return pl.pallas_call(
        kern,
        grid_spec=grid_spec,
        out_shape=[jax.ShapeDtypeStruct((m_rows, d_model), BF16), jax.ShapeDtypeStruct((ms, d_model), F32)],
        compiler_params=_params("arbitrary"),
    )(page_table, qb, kb, vt, qs, *([cache_k] * n_slots), *([cache_v] * n_slots), new_rows(kn), new_rows(vn),
      lamv, subln_col)


def _tile(n, target):
    t = min(n, target)
    assert n % t == 0
    return t


def kernel(x_prompt, x_sample, cache_k, cache_v, page_table, gm_w_in, gm_b_in, gm_ln_g, gm_ln_b, gm_w_s, gm_b_s,
           gm_w_out, gm_b_out, at_w_qkv, at_lambda_q1, at_lambda_k1, at_lambda_q2, at_lambda_k2, at_subln_g,
           at_w_out, ln_mix_g, ln_mix_b, ln_ffn_g, ln_ffn_b, ffn_w_in, ffn_w_out):
    n_batch, seq, d_model = x_prompt.shape
    n_dec, dec_seq, _ = x_sample.shape
    depth = ln_mix_g.shape[0]
    n_heads = d_model // HEAD_W
    n_pages = page_table.shape[1]
    past_len = n_pages * cache_k.shape[2]
    alpha = (2 * depth) ** 0.25
    inner = gm_w_in.shape[2] // 2
    d_ff = ffn_w_out.shape[1]

    xp = x_prompt.reshape(n_batch * seq, d_model)
    xs = x_sample.reshape(n_dec * dec_seq, d_model)
    mp, ms = xp.shape[0], xs.shape[0]

    tm_p = _tile(seq, 1024)
    tm_r = _tile(seq, 512)
    tm_g = _tile(seq, 512)
    tn = _tile(inner, 512)
    tf = 256 if d_ff % 256 == 0 else d_ff

    tables_p = _rope_tables(jnp.arange(seq, dtype=jnp.int32))
    tables_s = _rope_tables(jnp.tile(past_len + jnp.arange(dec_seq, dtype=jnp.int32), n_dec))

    gm_v_p, gm_v_s, k_p, v_p, k_s, v_s = [], [], [], [], [], []
    for i in range(depth):
        j = i // 2
        if i % 2 == 0:
            bexp = jnp.repeat(gm_b_s[j].T, GROUP_W, axis=1)
            wexp = jnp.repeat(jnp.transpose(gm_w_s[j][:, :dec_seq, :dec_seq], (2, 1, 0)), GROUP_W, axis=2)
            gated_p, gv_p = _gmlp_in_prompt(xp, gm_w_in, gm_b_in, gm_ln_g, gm_ln_b, gm_w_s, bexp, j, seq,
                                            tm=tm_g, tn=tn)
            gated_s, gv_s = _gmlp_in_sample(xs, gm_w_in, gm_b_in, gm_ln_g, gm_ln_b, wexp, bexp[:dec_seq], j,
                                            dec_seq, tn=tn)
            gm_v_p.append(gv_p)
            gm_v_s.append(gv_s.reshape(n_dec, dec_seq, inner))
            mix_p = dict(a=gated_p, w=gm_w_out, bias=gm_b_out)
            mix_s = dict(a=gated_s, w=gm_w_out, bias=gm_b_out)
        else:
            lam_init = 0.8 - 0.6 * math.exp(-0.3 * i)
            lamv = jnp.stack([at_lambda_q1[j], at_lambda_k1[j], at_lambda_q2[j], at_lambda_k2[j]])
            subln = at_subln_g[j].reshape(HEAD_W, 1)
            scale = HEAD_DIM ** -0.5 * math.log2(math.e)
            tq = _tile(seq, 512)
            proj = functools.partial(_proj_rope, xp, xs, at_w_qkv, j, n_cols=d_model, tm=tm_r)
            qb, qs = proj(col0=0, tables=tables_p, tables_s=tables_s, pos_blocks=seq // tm_r,
                          out_dtypes=(BF16,), scale=scale)
            kf, kb, kn = proj(col0=d_model, tables=tables_p, tables_s=tables_s, pos_blocks=seq // tm_r,
                              out_dtypes=(F32, BF16), scale=1.0, after=qs)
            vf, vn, vt = proj(col0=2 * d_model, tables=None, tables_s=None, pos_blocks=1,
                              out_dtypes=(F32,), scale=1.0, vt=(seq, tq), after=qs)
            a_p, a_s = _attention(qb, kb, vt, qs, cache_k, cache_v, page_table, kn, vn, lamv, subln, j, n_batch, seq,
                                  dec_seq, lam_init=lam_init, tq=tq, n_slots=math.gcd(n_pages, 8))
            k_p.append(kf.reshape(n_batch, seq, n_heads, HEAD_W))
            v_p.append(vf.reshape(n_batch, seq, n_heads, HEAD_W))
            k_s.append(kn.reshape(n_dec, dec_seq, n_heads, HEAD_W))
            v_s.append(vn.reshape(n_dec, dec_seq, n_heads, HEAD_W))
            mix_p = dict(a=a_p, w=at_w_out, bias=None)
            mix_s = dict(a=a_s, w=at_w_out, bias=None)
        xp, xs = _proj_ln(mix_p["a"], mix_s["a"], mix_p["w"], j, mix_p["bias"], xp, xs, ln_mix_g, ln_mix_b, i,
                          alpha=alpha, tm=tm_r)
        xp, xs = _ffn(xp, xs, ffn_w_in, ffn_w_out, i, ln_ffn_g, ln_ffn_b, alpha=alpha, tm=tm_p, tf=tf)

    return (xp.reshape(n_batch, seq, d_model), xs.reshape(n_dec, dec_seq, d_model),
            jnp.stack(gm_v_p), jnp.stack(gm_v_s), jnp.stack(k_p), jnp.stack(v_p), jnp.stack(k_s), jnp.stack(v_s))
```

```python
import functools
import math

import jax
import jax.numpy as jnp
from jax import lax
from jax.experimental import pallas as pl
from jax.experimental.pallas import tpu as pltpu

F32 = jnp.float32
BF16 = jnp.bfloat16

LN_EPS = 1e-5
CHUNK = 128
GROUP_W = 128
HEAD_DIM = 64
HEAD_W = 2 * HEAD_DIM
ROT_DIM = HEAD_DIM // 4
ROPE_THETA = 500000.0
LANES = 128
SUBLANES = 8
MXU_COLS = 256
DENOM_ROWS = 16
LN_ROWS = 128
SOFTMAX_ROWS = 64
VMEM_LIMIT_BYTES = 60 * 1024 * 1024


def _params(*sem):
    return pltpu.CompilerParams(dimension_semantics=sem, vmem_limit_bytes=VMEM_LIMIT_BYTES)


def _layer_norm(y, g, b):
    mu = jnp.mean(y, axis=-1, keepdims=True)
    yc = y - mu
    var = jnp.mean(yc * yc, axis=-1, keepdims=True)
    return yc * lax.rsqrt(var + LN_EPS) * g + b


def _residual_ln_inplace(o_ref, x_ref, bias_ref, g_ref, b_ref, alpha, tm):
    block = min(tm, LN_ROWS)
    assert tm % block == 0

    def body(r, carry):
        rows = pl.ds(pl.multiple_of(r * block, block), block)
        y = alpha * x_ref[rows, :] + o_ref[rows, :]
        if bias_ref is not None:
            y = y + bias_ref[...]
        o_ref[rows, :] = _layer_norm(y, g_ref[...], b_ref[...])
        return carry

    lax.fori_loop(0, tm // block, body, 0)


def _cast_rows_to_bf16(dst_ref, src_ref):
    rows = 256
    n = src_ref.shape[0]
    assert n % rows == 0

    def body(r, carry):
        sl = pl.ds(pl.multiple_of(r * rows, rows), rows)
        dst_ref[sl, :] = src_ref[sl, :].astype(BF16)
        return carry

    lax.fori_loop(0, n // rows, body, 0)


def _gmlp_project(x_ref, w_ref, b_ref, xb_s, z_s, wb_s=None):
    j = pl.program_id(1)

    @pl.when(j == 0)
    def _():
        xb_s[...] = x_ref[...].astype(BF16)

    if wb_s is None:
        wb = w_ref[...].astype(BF16)
    else:
        @pl.when(pl.program_id(0) == 0)
        def _():
            wb_s[j] = w_ref[...].astype(BF16)

        wb = wb_s[j]
    z = jnp.dot(xb_s[...], wb, preferred_element_type=F32) + b_ref[...]
    z_s[j] = 0.5 * z * (1.0 + lax.erf(z * math.sqrt(0.5)))


def _gmlp_v_layer_norm(z_s, lng_ref, lnb_ref, rows, n_half, tn, inner):
    vs = [z_s[n_half + c, rows, :] for c in range(n_half)]
    mu = sum(jnp.sum(v, axis=-1, keepdims=True) for v in vs) / inner
    var = sum(jnp.sum(jnp.square(v - mu), axis=-1, keepdims=True) for v in vs) / inner
    rstd = lax.rsqrt(var + LN_EPS)
    return [(vs[c] - mu) * rstd * lng_ref[:, c * tn:(c + 1) * tn] + lnb_ref[:, c * tn:(c + 1) * tn]
            for c in range(n_half)]


def _gmlp_in_prompt_kernel(x_ref, w_ref, b_ref, lng_ref, lnb_ref, ws_ref, bexp_ref,
                           gated_ref, gmv_ref, xb_s, z_s, vn_s, wb_s, *, tm, tn, inner):
    _gmlp_project(x_ref, w_ref, b_ref, xb_s, z_s, wb_s)
    n_half = inner // tn
    n_groups = inner // GROUP_W
    n_rc = tm // CHUNK
    per_chunk = tn // GROUP_W

    @pl.when(pl.program_id(1) == pl.num_programs(1) - 1)
    def _():
        for r in range(n_rc):
            rows = slice(r * CHUNK, (r + 1) * CHUNK)
            vn = _gmlp_v_layer_norm(z_s, lng_ref, lnb_ref, rows, n_half, tn, inner)
            for c in range(n_half):
                if r == n_rc - 1:
                    gmv_ref[:, c * tn:(c + 1) * tn] = vn[c]
                for q in range(per_chunk):
                    g = c * per_chunk + q
                    vn_s[g, :, r * CHUNK:(r + 1) * CHUNK] = vn[c][:, q * GROUP_W:(q + 1) * GROUP_W].astype(BF16)
        t_idx = lax.broadcasted_iota(jnp.int32, (CHUNK, CHUNK), 0)
        s_idx = lax.broadcasted_iota(jnp.int32, (CHUNK, CHUNK), 1)
        causal = s_idx <= t_idx
        for g in range(n_groups):
            w_causal = jnp.where(causal, ws_ref[g], 0.0).astype(BF16)
            mixed = jnp.dot(w_causal, vn_s[g], preferred_element_type=F32)
            cols = slice(g * GROUP_W, (g + 1) * GROUP_W)
            c, q = divmod(g, per_chunk)
            for r in range(n_rc):
                rows = slice(r * CHUNK, (r + 1) * CHUNK)
                u = z_s[c, rows, q * GROUP_W:(q + 1) * GROUP_W]
                m = mixed[:, r * CHUNK:(r + 1) * CHUNK] + bexp_ref[:, cols]
                gated_ref[rows, cols] = (u * m).astype(BF16)


def _gmlp_in_prompt(x, w_in, b_in, ln_g, ln_b, w_s, bexp, layer, seq, *, tm, tn):
    m_rows, d_model = x.shape
    inner = w_in.shape[2] // 2
    n_groups = inner // GROUP_W
    n_batch = m_rows // seq
    assert seq % tm == 0 and tm % CHUNK == 0 and inner % tn == 0 and tn % GROUP_W == 0
    tiles_per_seq = seq // tm
    nj = 2 * inner // tn
    kern = functools.partial(_gmlp_in_prompt_kernel, tm=tm, tn=tn, inner=inner)
    return pl.pallas_call(
        kern,
        grid=(m_rows // tm, nj),
        in_specs=[
            pl.BlockSpec((tm, d_model), lambda i, j: (i, 0)),
            pl.BlockSpec((None, d_model, tn), lambda i, j: (layer, 0, jnp.where(i == 0, j, nj - 1))),
            pl.BlockSpec((None, 1, tn), lambda i, j: (layer, 0, j)),
            pl.BlockSpec((None, 1, inner), lambda i, j: (layer, 0, 0)),
            pl.BlockSpec((None, 1, inner), lambda i, j: (layer, 0, 0)),
            pl.BlockSpec((None, n_groups, CHUNK, CHUNK), lambda i, j: (layer, 0, 0, 0)),
            pl.BlockSpec((CHUNK, inner), lambda i, j: (0, 0)),
        ],
        out_specs=[
            pl.BlockSpec((tm, inner), lambda i, j: (i, 0)),
            pl.BlockSpec((None, CHUNK, inner), lambda i, j: (i // tiles_per_seq, 0, 0)),
        ],
        out_shape=[
            jax.ShapeDtypeStruct((m_rows, inner), BF16),
            jax.ShapeDtypeStruct((n_batch, CHUNK, inner), F32),
        ],
        scratch_shapes=[
            pltpu.VMEM((tm, d_model), BF16),
            pltpu.VMEM((nj, tm, tn), F32),
            pltpu.VMEM((n_groups, CHUNK, tm), BF16),
            pltpu.VMEM((nj, d_model, tn), BF16),
        ],
        compiler_params=_params("arbitrary", "arbitrary"),
    )(x, w_in, b_in.reshape(b_in.shape[0], 1, -1), ln_g.reshape(ln_g.shape[0], 1, -1),
      ln_b.reshape(ln_b.shape[0], 1, -1), w_s, bexp)


def _gmlp_in_sample_kernel(x_ref, w_ref, b_ref, lng_ref, lnb_ref, wexp_ref, bexp_ref,
                           gated_ref, gmv_ref, xb_s, z_s, *, tm, tn, inner, dec_seq):
    _gmlp_project(x_ref, w_ref, b_ref, xb_s, z_s)
    n_half = inner // tn

    @pl.when(pl.program_id(1) == pl.num_programs(1) - 1)
    def _():
        vn = _gmlp_v_layer_norm(z_s, lng_ref, lnb_ref, slice(0, tm), n_half, tn, inner)
        t_idx = lax.broadcasted_iota(jnp.int32, (dec_seq, tn), 0)
        for c in range(n_half):
            cols = slice(c * tn, (c + 1) * tn)
            gmv_ref[:, cols] = vn[c]
            for b in range(tm // dec_seq):
                rows = slice(b * dec_seq, (b + 1) * dec_seq)
                vb = vn[c][rows, :]
                mixed = bexp_ref[:, cols]
                for s in range(dec_seq):
                    w_ts = jnp.where(t_idx >= s, wexp_ref[s, :, cols], 0.0)
                    mixed = mixed + w_ts * vb[s:s + 1, :]
                gated_ref[rows, cols] = (z_s[c, rows, :] * mixed).astype(BF16)


def _gmlp_in_sample(x, w_in, b_in, ln_g, ln_b, wexp, bexp, layer, dec_seq, *, tn):
    m_rows, d_model = x.shape
    inner = w_in.shape[2] // 2
    assert dec_seq % 8 == 0 and dec_seq <= CHUNK and inner % tn == 0
    nj = 2 * inner // tn
    kern = functools.partial(_gmlp_in_sample_kernel, tm=m_rows, tn=tn, inner=inner, dec_seq=dec_seq)
    return pl.pallas_call(
        kern,
        grid=(1, nj),
        in_specs=[
            pl.BlockSpec((m_rows, d_model), lambda i, j: (0, 0)),
            pl.BlockSpec((None, d_model, tn), lambda i, j: (layer, 0, j)),
            pl.BlockSpec((None, 1, tn), lambda i, j: (layer, 0, j)),
            pl.BlockSpec((None, 1, inner), lambda i, j: (layer, 0, 0)),
            pl.BlockSpec((None, 1, inner), lambda i, j: (layer, 0, 0)),
            pl.BlockSpec((dec_seq, dec_seq, inner), lambda i, j: (0, 0, 0)),
            pl.BlockSpec((dec_seq, inner), lambda i, j: (0, 0)),
        ],
        out_specs=[
            pl.BlockSpec((m_rows, inner), lambda i, j: (0, 0)),
            pl.BlockSpec((m_rows, inner), lambda i, j: (0, 0)),
        ],
        out_shape=[
            jax.ShapeDtypeStruct((m_rows, inner), BF16),
            jax.ShapeDtypeStruct((m_rows, inner), F32),
        ],
        scratch_shapes=[
            pltpu.VMEM((m_rows, d_model), BF16),
            pltpu.VMEM((nj, m_rows, tn), F32),
        ],
        compiler_params=_params("arbitrary", "arbitrary"),
    )(x, w_in, b_in.reshape(b_in.shape[0], 1, -1), ln_g.reshape(ln_g.shape[0], 1, -1),
      ln_b.reshape(ln_b.shape[0], 1, -1), wexp, bexp)


def _proj_ln_kernel(*refs, alpha, has_bias):
    if has_bias:
        a_ref, as_ref, w_ref, bias_ref, x_ref, xs_ref, g_ref, b_ref, o_ref, os_ref, wb_s = refs
    else:
        a_ref, as_ref, w_ref, x_ref, xs_ref, g_ref, b_ref, o_ref, os_ref, wb_s = refs
        bias_ref = None

    def apply(a, x_in, o):
        o[...] = jnp.dot(a[...].astype(BF16), wb_s[...], preferred_element_type=F32)
        _residual_ln_inplace(o, x_in, bias_ref, g_ref, b_ref, alpha, o.shape[0])

    @pl.when(pl.program_id(0) == 0)
    def _():
        _cast_rows_to_bf16(wb_s, w_ref)
        apply(as_ref, xs_ref, os_ref)

    apply(a_ref, x_ref, o_ref)


def _proj_ln(a, a_s, w, layer, bias, x, xs, ln_g, ln_b, ln_idx, *, alpha, tm):
    m_rows, k_dim = a.shape
    ms = a_s.shape[0]
    d_model = x.shape[1]
    assert m_rows % tm == 0
    vec = lambda idx: pl.BlockSpec((None, 1, d_model), lambda i: (idx, 0, 0))
    in_specs = [pl.BlockSpec((tm, k_dim), lambda i: (i, 0)),
                pl.BlockSpec((ms, k_dim), lambda i: (0, 0)),
                pl.BlockSpec((None, k_dim, d_model), lambda i: (layer, 0, 0), pipeline_mode=pl.Buffered(1))]
    args = [a, a_s, w]
    if bias is not None:
        in_specs.append(vec(layer))
        args.append(bias.reshape(bias.shape[0], 1, -1))
    in_specs += [pl.BlockSpec((tm, d_model), lambda i: (i, 0)), pl.BlockSpec((ms, d_model), lambda i: (0, 0)),
                 vec(ln_idx), vec(ln_idx)]
    args += [x, xs, ln_g.reshape(ln_g.shape[0], 1, -1), ln_b.reshape(ln_b.shape[0], 1, -1)]
    kern = functools.partial(_proj_ln_kernel, alpha=alpha, has_bias=bias is not None)
    return pl.pallas_call(
        kern,
        grid=(m_rows // tm,),
        in_specs=in_specs,
        out_specs=[pl.BlockSpec((tm, d_model), lambda i: (i, 0)), pl.BlockSpec((ms, d_model), lambda i: (0, 0))],
        out_shape=[jax.ShapeDtypeStruct((m_rows, d_model), F32), jax.ShapeDtypeStruct((ms, d_model), F32)],
        scratch_shapes=[pltpu.VMEM((k_dim, d_model), BF16)],
        compiler_params=_params("arbitrary"),
    )(*args)


def _ffn_kernel(x_ref, xs_ref, wg_ref, wu_ref, wo_ref, g_ref, b_ref, o_ref, os_ref, xb_s, *, alpha, tm):
    i, f = pl.program_id(0), pl.program_id(1)
    last = pl.num_programs(1) - 1

    @pl.when(f == 0)
    def _():
        xb_s[0:tm, :] = x_ref[...].astype(BF16)
        o_ref[...] = jnp.zeros_like(o_ref)

    @pl.when(jnp.logical_and(f == 0, i == 0))
    def _():
        xb_s[tm:, :] = xs_ref[...].astype(BF16)
        os_ref[...] = jnp.zeros_like(os_ref)

    def swiglu(xb):
        gate = jnp.dot(xb, wg_ref[...].astype(BF16), preferred_element_type=F32)
        up = jnp.dot(xb, wu_ref[...].astype(BF16), preferred_element_type=F32)
        h = (jax.nn.silu(gate) * up).astype(BF16)
        return jnp.dot(h, wo_ref[...].astype(BF16), preferred_element_type=F32)

    @pl.when(i == 0)
    def _():
        part = swiglu(xb_s[...])
        o_ref[...] += part[0:tm]
        os_ref[...] += part[tm:]

    @pl.when(i > 0)
    def _():
        o_ref[...] += swiglu(xb_s[0:tm, :])

    @pl.when(f == last)
    def _():
        _residual_ln_inplace(o_ref, x_ref, None, g_ref, b_ref, alpha, tm)

    @pl.when(jnp.logical_and(f == last, i == 0))
    def _():
        _residual_ln_inplace(os_ref, xs_ref, None, g_ref, b_ref, alpha, os_ref.shape[0])


def _ffn(x, xs, w_in, w_out, layer, ln_g, ln_b, *, alpha, tm, tf):
    m_rows, d_model = x.shape
    ms = xs.shape[0]
    d_ff = w_out.shape[1]
    assert m_rows % tm == 0 and d_ff % tf == 0
    nf = d_ff // tf
    vec = pl.BlockSpec((None, 1, d_model), lambda i, f: (layer, 0, 0))
    kern = functools.partial(_ffn_kernel, alpha=alpha, tm=tm)
    return pl.pallas_call(
        kern,
        grid=(m_rows // tm, nf),
        in_specs=[
            pl.BlockSpec((tm, d_model), lambda i, f: (i, 0)),
            pl.BlockSpec((ms, d_model), lambda i, f: (0, 0)),
            pl.BlockSpec((None, d_model, tf), lambda i, f: (layer, 0, f)),
            pl.BlockSpec((None, d_model, tf), lambda i, f: (layer, 0, nf + f)),
            pl.BlockSpec((None, tf, d_model), lambda i, f: (layer, f, 0)),
            vec, vec,
        ],
        out_specs=[pl.BlockSpec((tm, d_model), lambda i, f: (i, 0)),
                   pl.BlockSpec((ms, d_model), lambda i, f: (0, 0))],
        out_shape=[jax.ShapeDtypeStruct((m_rows, d_model), F32), jax.ShapeDtypeStruct((ms, d_model), F32)],
        scratch_shapes=[pltpu.VMEM((tm + ms, d_model), BF16)],
        compiler_params=_params("arbitrary", "arbitrary"),
    )(x, xs, w_in, w_in, w_out, ln_g.reshape(ln_g.shape[0], 1, -1), ln_b.reshape(ln_b.shape[0], 1, -1))


def _rope_tables(pos):
    half = ROT_DIM // 2
    inv = jnp.power(ROPE_THETA, -jnp.arange(half, dtype=F32) * 2.0 / ROT_DIM)
    ang = pos.astype(F32)[:, None] * inv[None, :]
    cos, sin = jnp.cos(ang), jnp.sin(ang)
    n = pos.shape[0]
    rest = HEAD_DIM - ROT_DIM
    c = jnp.concatenate([cos, cos, jnp.ones((n, rest), F32)], axis=1)
    s_next = jnp.concatenate([-sin, jnp.zeros((n, half + rest), F32)], axis=1)
    s_prev = jnp.concatenate([jnp.zeros((n, half), F32), sin, jnp.zeros((n, rest), F32)], axis=1)
    return tuple(jnp.tile(t, (1, HEAD_W // HEAD_DIM)) for t in (c, s_next, s_prev))


def _proj_rope_kernel(*refs, rope, scale, n_out, vt_tk):
    n_tab = 3 if rope else 0
    x_ref, xs_ref, w_ref = refs[:3]
    tabs, tabs_s = refs[3:3 + n_tab], refs[3 + n_tab:3 + 2 * n_tab]
    k = 3 + 2 * n_tab
    out_refs, os_ref = refs[k:k + n_out], refs[k + n_out]
    vt_ref = refs[k + n_out + 1] if vt_tk else None
    wb_s = refs[-1]

    def emit(x_in, tables, outs, vt):
        y = jnp.dot(x_in[...].astype(BF16), wb_s[...], preferred_element_type=F32)
        for h in range(y.shape[1] // HEAD_W):
            cols = slice(h * HEAD_W, (h + 1) * HEAD_W)
            yh = y[:, cols]
            if rope:
                c_ref, sn_ref, sp_ref = tables
                half = ROT_DIM // 2
                yh = (yh * c_ref[...] + pltpu.roll(yh, HEAD_W - half, 1) * sn_ref[...]
                      + pltpu.roll(yh, half, 1) * sp_ref[...])
            if scale != 1.0:
                yh = yh * scale
            for o_ref in outs:
                o_ref[:, cols] = yh.astype(o_ref.dtype)
            if vt is not None:
                for kb in range(y.shape[0] // vt_tk):
                    vt[h, kb] = yh[kb * vt_tk:(kb + 1) * vt_tk, :].T.astype(BF16)

    @pl.when(pl.program_id(0) == 0)
    def _():
        _cast_rows_to_bf16(wb_s, w_ref)
        emit(xs_ref, tabs_s, (os_ref,), None)

    emit(x_ref, tabs, out_refs, vt_ref)


def _proj_rope(x, xs, w, layer, col0, n_cols, tables, tables_s, pos_blocks, out_dtypes, *, scale, tm, vt=None):
    m_rows, d_model = x.shape
    ms = xs.shape[0]
    assert m_rows % tm == 0 and col0 % n_cols == 0 and n_cols % HEAD_W == 0
    rope = tables is not None
    in_specs = [pl.BlockSpec((tm, d_model), lambda i: (i, 0)),
                pl.BlockSpec((ms, d_model), lambda i: (0, 0)),
                pl.BlockSpec((None, d_model, n_cols), lambda i: (layer, 0, col0 // n_cols),
                             pipeline_mode=pl.Buffered(1))]
    args = [x, xs, w]
    if rope:
        in_specs += [pl.BlockSpec((tm, HEAD_W), lambda i: (i % pos_blocks, 0))] * 3
        in_specs += [pl.BlockSpec((ms, HEAD_W), lambda i: (0, 0))] * 3
        args += list(tables) + list(tables_s)
    out_specs =[pl.BlockSpec((tm, n_cols), lambda i: (i, 0)) for _ in out_dtypes]
    out_shape = [jax.ShapeDtypeStruct((m_rows, n_cols), dt) for dt in out_dtypes]
    out_specs.append(pl.BlockSpec((ms, n_cols), lambda i: (0, 0)))
    out_shape.append(jax.ShapeDtypeStruct((ms, n_cols), F32))
    vt_tk = 0
    if vt is not None:
        seq, vt_tk = vt
        assert seq % tm == 0 and tm % vt_tk == 0
        tiles_per_seq, n_heads = seq // tm, n_cols // HEAD_W
        out_specs.append(pl.BlockSpec((n_heads, tm // vt_tk, HEAD_W, vt_tk),
                                      lambda i: (i // tiles_per_seq, i % tiles_per_seq, 0, 0)))
        out_shape.append(jax.ShapeDtypeStruct(((m_rows // seq) * n_heads, seq // vt_tk, HEAD_W, vt_tk), BF16))
    kern = functools.partial(_proj_rope_kernel, rope=rope, scale=scale, n_out=len(out_dtypes), vt_tk=vt_tk)
    return pl.pallas_call(
        kern,
        grid=(m_rows // tm,),
        in_specs=in_specs,
        out_specs=out_specs,
        out_shape=out_shape,
        scratch_shapes=[pltpu.VMEM((d_model, n_cols), BF16)],
        compiler_params=_params("arbitrary"),
    )(*args)


def _diff_lambda(lam_ref, lam_init):
    lv = lam_ref[...]
    e1 = jnp.exp(jnp.sum(lv[0:1, :] * lv[1:2, :], axis=-1, keepdims=True))
    e2 = jnp.exp(jnp.sum(lv[2:3, :] * lv[3:4, :], axis=-1, keepdims=True))
    return e1 - e2 + lam_init


def _map_masks(rows):
    lane = lax.broadcasted_iota(jnp.int32, (rows, HEAD_W), 1)
    return lane < HEAD_DIM, lane >= HEAD_DIM


def _sub_ln(o, g_col, lam_init):
    return o * lax.rsqrt(jnp.mean(o * o, axis=0, keepdims=True) + LN_EPS) * g_col * (1.0 - lam_init)


def _prompt_attention(q_ref, k_ref, vt_ref, lam_ref, g_ref, o_ref, q2_s, m_s, acc_s, c_s, mx_s, s_s, p_s,
                      *, tq, lam_init, q_steps):
    gw = min(MXU_COLS, tq)
    n_g = 2 * tq // gw
    m0, m1 = _map_masks(tq)
    lam = _diff_lambda(lam_ref, lam_init)

    def block(ki, diagonal, buf):
        kb = k_ref[ki * tq:(ki + 1) * tq, :]
        vtb = jnp.concatenate([vt_ref[ki], jnp.ones((DENOM_ROWS, tq), BF16)], axis=0)
        n_keys = [min(tq, (g * gw) % tq + gw) if diagonal else tq for g in range(n_g)]
        for g in range(n_g):
            cols, nk = slice(g * gw, (g + 1) * gw), n_keys[g]
            s = lax.dot_general(kb[:nk], q2_s[cols, :], (((1,), (1,)), ((), ())), preferred_element_type=F32)
            if diagonal:
                key = lax.broadcasted_iota(jnp.int32, (nk, gw), 0)
                qry = lax.broadcasted_iota(jnp.int32, (nk, gw), 1) + (g * gw) % tq
                s = jnp.where(key <= qry, s, -jnp.inf)
            s_s[buf, g, 0:nk, :] = s
            mx_s[:, cols] = jnp.max(s, axis=0, keepdims=True)
        for g in range(n_g):
            cols, nk = slice(g * gw, (g + 1) * gw), n_keys[g]
            m_old = m_s[:, cols]
            m_new = jnp.maximum(m_old, mx_s[:, cols])
            c_s[:, cols] = jnp.exp2(m_old - m_new)
            for r in range(0, nk, SOFTMAX_ROWS):
                p_s[buf, g, r:r + SOFTMAX_ROWS, :] = jnp.exp2(s_s[buf, g, r:r + SOFTMAX_ROWS, :] - m_new).astype(BF16)
            m_s[:, cols] = m_new
        for g in range(n_g):
            cols, nk = slice(g * gw, (g + 1) * gw), n_keys[g]
            acc_s[:, cols] = c_s[:, cols] * acc_s[:, cols] + jnp.dot(vtb[:, :nk], p_s[buf, g, 0:nk, :],
                                                                     preferred_element_type=F32)

    n_blocks = 0
    for qi in q_steps:
        rows = slice(qi * tq, (qi + 1) * tq)
        q = q_ref[rows, :]
        zero = jnp.zeros_like(q)
        q2_s[0:tq, :] = jnp.where(m0, q, zero)
        q2_s[tq:2 * tq, :] = jnp.where(m1, q, zero)
        m_s[...] = jnp.full_like(m_s, -jnp.inf)
        acc_s[...] = jnp.zeros_like(acc_s)
        for ki in range(qi + 1):
            block(ki, ki == qi, n_blocks % 2)
            n_blocks += 1
        o = acc_s[0:HEAD_W, :] / acc_s[HEAD_W:HEAD_W + 1, :]
        o = o[:, :tq] - lam * o[:, tq:]
        o_ref[rows, :] = _sub_ln(o, g_ref[...], lam_init).T.astype(o_ref.dtype)


def _sample_attention(j, n_steps, q_ref, k_pages, v_pages, kn_ref, vn_ref, lam_ref, g_ref, o_ref,
                      qm_s, m_s, l_s, acc_s, *, n_heads, dec_seq, lam_init):
    n_slots = len(k_pages)
    hp = min(n_heads, SUBLANES)
    n_groups = n_heads // hp
    cw = 2 * dec_seq
    width = hp * cw

    @pl.when(j == 0)
    def _():
        m0, m1 = _map_masks(dec_seq)
        for h in range(n_heads):
            grp, hl = divmod(h, hp)
            qh = q_ref[:, h * HEAD_W:(h + 1) * HEAD_W]
            qm_s[grp, hl * cw:(hl + 1) * cw, :] = jnp.concatenate(
                [jnp.where(m0, qh, 0.0), jnp.where(m1, qh, 0.0)], axis=0).astype(BF16)
        m_s[...] = jnp.full_like(m_s, -jnp.inf)
        l_s[...] = jnp.zeros_like(l_s)
        acc_s[...] = jnp.zeros_like(acc_s)

    own = (lax.broadcasted_iota(jnp.int32, (hp, width), 1) // cw
           == lax.broadcasted_iota(jnp.int32, (hp, width), 0))

    def update(slot, k3_ref, v3_ref, causal):
        n_keys = k3_ref.shape[0]
        rows = n_keys * hp
        for grp in range(n_groups):
            heads = slice(grp * hp, (grp + 1) * hp)
            kr = k3_ref[:, heads, :].reshape(rows, HEAD_W).astype(BF16)
            u = lax.dot_general(kr, qm_s[grp], (((1,), (1,)), ((), ())), preferred_element_type=F32)
            u = u.reshape(n_keys, hp, width)
            if causal:
                key = lax.broadcasted_iota(jnp.int32, (n_keys, hp, width), 0)
                qry = lax.broadcasted_iota(jnp.int32, (n_keys, hp, width), 2) % dec_seq
                u = jnp.where(key <= qry, u, -jnp.inf)
            m_old = m_s[slot, grp]
            m_new = jnp.maximum(m_old, jnp.max(u, axis=0))
            p = jnp.where(own, jnp.exp2(u - m_new), 0.0)
            corr = jnp.exp2(m_old - m_new)
            l_s[slot, grp] = corr * l_s[slot, grp] + jnp.sum(p, axis=0)
            m_s[slot, grp] = m_new
            vr = v3_ref[:, heads, :].reshape(rows, HEAD_W).astype(BF16)
            pv = lax.dot_general(vr, p.reshape(rows, width).astype(BF16), (((0,), (0,)), ((), ())),
                                 preferred_element_type=F32)
            corr_col = jnp.sum(jnp.where(own, corr, 0.0), axis=0, keepdims=True)
            acc_s[slot, grp] = acc_s[slot, grp] * corr_col + pv

    for slot in range(n_slots):
        update(slot, k_pages[slot], v_pages[slot], False)

    @pl.when(j == n_steps - 1)
    def _():
        update(0, kn_ref, vn_ref, True)
        lam = _diff_lambda(lam_ref, lam_init)
        for grp in range(n_groups):
            m_all = m_s[0, grp]
            for slot in range(1, n_slots):
                m_all = jnp.maximum(m_all, m_s[slot, grp])
            l_all = jnp.zeros((hp, width), F32)
            acc = jnp.zeros((HEAD_W, width), F32)
            for slot in range(n_slots):
                w = jnp.exp2(m_s[slot, grp] - m_all)
                l_all = l_all + w * l_s[slot, grp]
                acc = acc + acc_s[slot, grp] * jnp.sum(jnp.where(own, w, 0.0), axis=0, keepdims=True)
            o = acc / jnp.sum(jnp.where(own, l_all, 0.0), axis=0, keepdims=True)
            o = o - lam * pltpu.roll(o, width - dec_seq, 1)
            ot = _sub_ln(o, g_ref[...], lam_init).T
            for hl in range(hp):
                h = grp * hp + hl
                o_ref[:, h * HEAD_W:(h + 1) * HEAD_W] = ot[hl * cw:hl * cw + dec_seq, :]


def _attn_kernel(pt_ref, qp_ref, kp_ref, vtp_ref, qs_ref, *refs, n_slots, n_heads, dec_seq, tq, lam_init,
                 q_split, prompt_steps, sample_steps, steps_per_seq):
    del pt_ref
    k_pages, v_pages = refs[:n_slots], refs[n_slots:2 * n_slots]
    kn_ref, vn_ref, lam_ref, g_ref, op_ref, os_ref = refs[2 * n_slots:2 * n_slots + 6]
    prompt_scratch, sample_scratch = refs[2 * n_slots + 6:-4], refs[-4:]
    t = pl.program_id(0)

    for part, q_steps in enumerate(q_split):
        @pl.when(jnp.logical_and(t < prompt_steps, t % len(q_split) == part))
        def _():
            _prompt_attention(qp_ref, kp_ref, vtp_ref, lam_ref, g_ref, op_ref, *prompt_scratch,
                              tq=tq, lam_init=lam_init, q_steps=q_steps)

    @pl.when(t < sample_steps)
    def _():
        _sample_attention(t % steps_per_seq, steps_per_seq, qs_ref, k_pages, v_pages, kn_ref, vn_ref, lam_ref, g_ref,
                          os_ref, *sample_scratch, n_heads=n_heads, dec_seq=dec_seq, lam_init=lam_init)


def _attention(qb, kb, vt, qs, cache_k, cache_v, page_table, kn, vn, lamv, subln_col, layer, n_batch, seq, dec_seq,
               *, lam_init, tq, n_slots):
    m_rows, d_model = qb.shape
    ms = qs.shape[0]
    n_heads = d_model // HEAD_W
    n_dec, n_pages = page_table.shape
    page = cache_k.shape[2]
    hp = min(n_heads, SUBLANES)
    nq = seq // tq
    gw = min(MXU_COLS, tq)
    assert seq % tq == 0 and tq % SOFTMAX_ROWS == 0 and vt.shape[1:] == (nq, HEAD_W, tq)
    assert n_pages % n_slots == 0 and dec_seq % SUBLANES == 0 and n_heads % hp == 0
    width = hp * 2 * dec_seq
    n_groups = n_heads // hp
    q_split = [tuple(range(nq - 1)), (nq - 1,)] if nq > 1 else [(0,)]
    n_units, parts = n_batch * n_heads, len(q_split)
    steps_per_seq = n_pages // n_slots
    prompt_steps, sample_steps = n_units * parts, n_dec * steps_per_seq
    n_steps = max(prompt_steps, sample_steps)

    def unit(t):
        return jnp.minimum(t // parts, n_units - 1)

    def seq_of(t):
        return jnp.minimum(t // steps_per_seq, n_dec - 1)

    def page_spec(slot):
        def index(t, pt):
            j = jnp.where(t < sample_steps, t % steps_per_seq, steps_per_seq - 1)
            return (layer, pt[seq_of(t), j * n_slots + slot], 0, 0, 0)
        return pl.BlockSpec((None, None, page, n_heads, HEAD_W), index)

    head_spec = pl.BlockSpec((seq, HEAD_W), lambda t, pt: (unit(t) // n_heads, unit(t) % n_heads))
    row_spec = pl.BlockSpec((dec_seq, d_model), lambda t, pt: (seq_of(t), 0))
    new_spec = pl.BlockSpec((None, dec_seq, n_heads, HEAD_W), lambda t, pt: (seq_of(t), 0, 0, 0))
    kern = functools.partial(_attn_kernel, n_slots=n_slots, n_heads=n_heads, dec_seq=dec_seq, tq=tq,
                             lam_init=lam_init, q_split=q_split, prompt_steps=prompt_steps,
                             sample_steps=sample_steps, steps_per_seq=steps_per_seq)
    grid_spec = pltpu.PrefetchScalarGridSpec(
        num_scalar_prefetch=1,
        grid=(n_steps,),
        in_specs=([head_spec, head_spec,
                   pl.BlockSpec((None, nq, HEAD_W, tq), lambda t, pt: (unit(t), 0, 0, 0)),
                   row_spec]
                  + [page_spec(s) for s in range(n_slots)] * 2 + [new_spec, new_spec]
                  + [pl.BlockSpec((4, HEAD_DIM), lambda t, pt: (0, 0)),
                     pl.BlockSpec((HEAD_W, 1), lambda t, pt: (0, 0))]),
        out_specs=[head_spec, row_spec],
        scratch_shapes=[
            pltpu.VMEM((2 * tq, HEAD_W), BF16),
            pltpu.VMEM((1, 2 * tq), F32),
            pltpu.VMEM((HEAD_W + DENOM_ROWS, 2 * tq), F32),
            pltpu.VMEM((1, 2 * tq), F32),
            pltpu.VMEM((1, 2 * tq), F32),
            pltpu.VMEM((2, 2 * tq // gw, tq, gw), F32),
            pltpu.VMEM((2, 2 * tq // gw, tq, gw), BF16),
            pltpu.VMEM((n_groups, width, HEAD_W), BF16),
            pltpu.VMEM((n_slots, n_groups, hp, width), F32),
            pltpu.VMEM((n_slots, n_groups, hp, width), F32),
            pltpu.VMEM((n_slots, n_groups, HEAD_W, width), F32),
        ],
    )
    new_rows = lambda a: a.reshape(n_dec, dec_seq, n_heads, HEAD_W)
    return pl.pallas_call(
        kern,
        grid_spec=grid_spec,
        out_shape=[jax.ShapeDtypeStruct((m_rows, d_model), BF16), jax.ShapeDtypeStruct((ms, d_model), F32)],
        compiler_params=_params("arbitrary"),
    )(page_table, qb, kb, vt, qs, *([cache_k] * n_slots), *([cache_v] * n_slots), new_rows(kn), new_rows(vn),
      lamv, subln_col)


def _tile(n, target):
    t = min(n, target)
    assert n % t == 0
    return t


def kernel(x_prompt, x_sample, cache_k, cache_v, page_table, gm_w_in, gm_b_in, gm_ln_g, gm_ln_b, gm_w_s, gm_b_s,
           gm_w_out, gm_b_out, at_w_qkv, at_lambda_q1, at_lambda_k1, at_lambda_q2, at_lambda_k2, at_subln_g,
           at_w_out, ln_mix_g, ln_mix_b, ln_ffn_g, ln_ffn_b, ffn_w_in, ffn_w_out):
    n_batch, seq, d_model = x_prompt.shape
    n_dec, dec_seq, _ = x_sample.shape
    depth = ln_mix_g.shape[0]
    n_heads = d_model // HEAD_W
    n_pages = page_table.shape[1]
    past_len = n_pages * cache_k.shape[2]
    alpha = (2 * depth) ** 0.25
    inner = gm_w_in.shape[2] // 2
    d_ff = ffn_w_out.shape[1]

    xp = x_prompt.reshape(n_batch * seq, d_model)
    xs = x_sample.reshape(n_dec * dec_seq, d_model)
    mp, ms = xp.shape[0], xs.shape[0]

    tm_p = _tile(seq, 1024)
    tm_r = _tile(seq, 512)
    tm_g = _tile(seq, 512)
    tn = _tile(inner, 512)
    tf = 256 if d_ff % 256 == 0 else d_ff

    tables_p = _rope_tables(jnp.arange(seq, dtype=jnp.int32))
    tables_s = _rope_tables(jnp.tile(past_len + jnp.arange(dec_seq, dtype=jnp.int32), n_dec))

    gm_v_p, gm_v_s, k_p, v_p, k_s, v_s = [], [], [], [], [], []
    for i in range(depth):
        j = i // 2
        if i % 2 == 0:
            bexp = jnp.repeat(gm_b_s[j].T, GROUP_W, axis=1)
            wexp = jnp.repeat(jnp.transpose(gm_w_s[j][:, :dec_seq, :dec_seq], (2, 1, 0)), GROUP_W, axis=2)
            gated_p, gv_p = _gmlp_in_prompt(xp, gm_w_in, gm_b_in, gm_ln_g, gm_ln_b, gm_w_s, bexp, j, seq,
                                            tm=tm_g, tn=tn)
            gated_s, gv_s = _gmlp_in_sample(xs, gm_w_in, gm_b_in, gm_ln_g, gm_ln_b, wexp, bexp[:dec_seq], j,
                                            dec_seq, tn=tn)
            gm_v_p.append(gv_p)
            gm_v_s.append(gv_s.reshape(n_dec, dec_seq, inner))
            mix_p = dict(a=gated_p, w=gm_w_out, bias=gm_b_out)
            mix_s = dict(a=gated_s, w=gm_w_out, bias=gm_b_out)
        else:
            lam_init = 0.8 - 0.6 * math.exp(-0.3 * i)
            lamv = jnp.stack([at_lambda_q1[j], at_lambda_k1[j], at_lambda_q2[j], at_lambda_k2[j]])
            subln = at_subln_g[j].reshape(HEAD_W, 1)
            scale = HEAD_DIM ** -0.5 * math.log2(math.e)
            tq = _tile(seq, 512)
            proj = functools.partial(_proj_rope, xp, xs, at_w_qkv, j, n_cols=d_model, tm=tm_r)
            qb, qs = proj(col0=0, tables=tables_p, tables_s=tables_s, pos_blocks=seq // tm_r,
                          out_dtypes=(BF16,), scale=scale)
            kf, kb, kn = proj(col0=d_model, tables=tables_p, tables_s=tables_s, pos_blocks=seq // tm_r,
                              out_dtypes=(F32, BF16), scale=1.0)
            vf, vn, vt = proj(col0=2 * d_model, tables=None, tables_s=None, pos_blocks=1,
                              out_dtypes=(F32,), scale=1.0, vt=(seq, tq))
            a_p, a_s = _attention(qb, kb, vt, qs, cache_k, cache_v, page_table, kn, vn, lamv, subln, j, n_batch, seq,
                                  dec_seq, lam_init=lam_init, tq=tq, n_slots=math.gcd(n_pages, 8))
            k_p.append(kf.reshape(n_batch, seq, n_heads, HEAD_W))
            v_p.append(vf.reshape(n_batch, seq, n_heads, HEAD_W))
            k_s.append(kn.reshape(n_dec, dec_seq, n_heads, HEAD_W))
            v_s.append(vn.reshape(n_dec, dec_seq, n_heads, HEAD_W))
            mix_p = dict(a=a_p, w=at_w_out, bias=None)
            mix_s = dict(a=a_s, w=at_w_out, bias=None)
        xp, xs = _proj_ln(mix_p["a"], mix_s["a"], mix_p["w"], j, mix_p["bias"], xp, xs, ln_mix_g, ln_mix_b, i,
                          alpha=alpha, tm=tm_r)
        xp, xs = _ffn(xp, xs, ffn_w_in, ffn_w_out, i, ln_ffn_g, ln_ffn_b, alpha=alpha, tm=tm_p, tf=tf)

    return (xp.reshape(n_batch, seq, d_model), xs.reshape(n_dec, dec_seq, d_model),
            jnp.stack(gm_v_p), jnp.stack(gm_v_s), jnp.stack(k_p), jnp.stack(v_p), jnp.stack(k_s), jnp.stack(v_s))
```
